```python
import math
import jax, jax.numpy as jnp
from jax import lax
import numpy as np

D_MODEL = 1024
BATCH = 4
SEQ = 8192
DEPTH = 2

CHUNK = 64
EPS = 1e-6
GDN_HEADS = 4
GDN_DK = 128
GDN_DV = 128
GDN_CONV = 4
GDN_QK = GDN_HEADS * GDN_DK
GDN_W = GDN_HEADS * GDN_DV
MLSTM_HEADS = 4
MLSTM_DH = 64
MLSTM_CONV = 4
MLSTM_W = MLSTM_HEADS * MLSTM_DH
S5_GROUPS = 16
S5_GROUP_CH = 16
S5_STATE = 64
S5_W = S5_GROUPS * S5_GROUP_CH
D_MIX = GDN_W + MLSTM_W + S5_W
D_FF = 2816
FFN_CONV = 3
IN_SPLITS = (GDN_QK, GDN_QK, GDN_W, GDN_W, GDN_HEADS, GDN_HEADS,
             MLSTM_W, MLSTM_W, MLSTM_W, MLSTM_W, MLSTM_HEADS, MLSTM_HEADS,
             S5_W)
D_IN = sum(IN_SPLITS)

kernel_name = 'hybrid_gdn_mlstm_s5_parallel_heads'


def rms_norm(x, w):
    x32 = x.astype(jnp.float32)
    y = x32 * lax.rsqrt(jnp.mean(x32 * x32, axis=-1, keepdims=True) + EPS) * w.astype(jnp.float32)
    return y.astype(x.dtype)


def head_rms_norm(x, w):
    return x * lax.rsqrt(jnp.mean(x * x, axis=-1, keepdims=True) + EPS) * w.astype(jnp.float32)


def l2_norm(x):
    return x * lax.rsqrt(jnp.sum(x * x, axis=-1, keepdims=True) + EPS)


def causal_depthwise_conv(x, w):
    k_w, ch = w.shape
    return lax.conv_general_dilated(
        x, w[:, None, :].astype(x.dtype), window_strides=(1,), padding=((k_w - 1, 0),),
        dimension_numbers=('NWC', 'WIO', 'NWC'), feature_group_count=ch)


def to_chunks(x):
    b_, l_, h_, d_ = x.shape
    return x.reshape(b_, l_ // CHUNK, CHUNK, h_, d_).transpose(1, 0, 3, 2, 4)


def gates_to_chunks(g):
    b_, l_, h_ = g.shape
    return g.reshape(b_, l_ // CHUNK, CHUNK, h_).transpose(1, 0, 3, 2)


def from_chunks(x):
    n_, b_, h_, c_, d_ = x.shape
    return x.transpose(1, 0, 3, 2, 4).reshape(b_, n_ * c_, h_, d_)


def gated_delta_rule_chunked(q, k, v, g, beta):
    b_, l_, h_, dk = q.shape
    dv = v.shape[-1]
    qc, kc, vc = to_chunks(q), to_chunks(k), to_chunks(v)
    gc = jnp.cumsum(gates_to_chunks(g), axis=-1)
    bc = gates_to_chunks(beta)
    tril = jnp.tril(jnp.ones((CHUNK, CHUNK), bool))
    strict = jnp.tril(jnp.ones((CHUNK, CHUNK), bool), k=-1)
    decay = jnp.exp(jnp.where(tril, gc[..., :, None] - gc[..., None, :], -jnp.inf))
    kb = kc * bc[..., None]
    m = jnp.where(strict, jnp.einsum('nbhtd,nbhsd->nbhts', kb, kc) * decay, 0.0)
    eye = jnp.eye(CHUNK, dtype=jnp.float32)
    t_inv = lax.linalg.triangular_solve(eye + m, jnp.broadcast_to(eye, m.shape),
                                        left_side=True, lower=True)
    u = jnp.einsum('nbhts,nbhsv->nbhtv', t_inv, vc * bc[..., None])
    w = jnp.einsum('nbhts,nbhsk->nbhtk', t_inv, kb * jnp.exp(gc)[..., None])
    a_qk = jnp.einsum('nbhtd,nbhsd->nbhts', qc, kc) * decay
    q_dec = qc * jnp.exp(gc)[..., None]
    k_dec = kc * jnp.exp(gc[..., -1:] - gc)[..., None]
    g_last = jnp.exp(gc[..., -1])

    def step(s, xs):
        q_i, k_i, u_i, w_i, a_i, gl = xs
        v_new = u_i - jnp.einsum('bhtk,bhkv->bhtv', w_i, s)
        o = jnp.einsum('bhtk,bhkv->bhtv', q_i, s) + jnp.einsum('bhts,bhsv->bhtv', a_i, v_new)
        s = s * gl[..., None, None] + jnp.einsum('bhsk,bhsv->bhkv', k_i, v_new)
        return s, o

    s0 = jnp.zeros((b_, h_, dk, dv), jnp.float32)
    _, o = lax.scan(step, s0, (q_dec, k_dec, u, w, a_qk, g_last))
    return from_chunks(o)


def gated_deltanet(q, k, v, z, b_pre, a_pre, conv_w, a_log, dt_bias, norm_w):
    b_, l_, _ = q.shape
    qkv = jax.nn.silu(causal_depthwise_conv(jnp.concatenate([q, k, v], axis=-1), conv_w))
    q, k, v = jnp.split(qkv.astype(jnp.float32), [GDN_QK, 2 * GDN_QK], axis=-1)
    q = l2_norm(q.reshape(b_, l_, GDN_HEADS, GDN_DK)) * (GDN_DK ** -0.5)
    k = l2_norm(k.reshape(b_, l_, GDN_HEADS, GDN_DK))
    v = v.reshape(b_, l_, GDN_HEADS, GDN_DV)
    beta = jax.nn.sigmoid(b_pre.astype(jnp.float32))
    g = -jnp.exp(a_log.astype(jnp.float32)) * jax.nn.softplus(
        a_pre.astype(jnp.float32) + dt_bias.astype(jnp.float32))
    o = gated_delta_rule_chunked(q, k, v, g, beta)
    o = head_rms_norm(o, norm_w) * jax.nn.silu(
        z.astype(jnp.float32).reshape(b_, l_, GDN_HEADS, GDN_DV))
    return o.reshape(b_, l_, GDN_W)


def mlstm_chunked(q, k, v, i_gate, log_f):
    b_, l_, h_, d_ = q.shape
    qc, kc, vc = to_chunks(q), to_chunks(k), to_chunks(v)
    ic = gates_to_chunks(i_gate)
    bcum = jnp.cumsum(gates_to_chunks(log_f), axis=-1)
    tril = jnp.tril(jnp.ones((CHUNK, CHUNK), bool))
    log_w = jnp.where(tril, bcum[..., :, None] - bcum[..., None, :] + ic[..., None, :], -jnp.inf)
    m_intra = jnp.max(log_w, axis=-1)
    qk = jnp.einsum('nbhtd,nbhsd->nbhts', qc, kc)
    log_end = bcum[..., -1:] - bcum + ic
    m_end = jnp.max(log_end, axis=-1)

    def step(carry, xs):
        c_st, n_st, m_st = carry
        q_i, k_i, v_i, b_i, lw_i, mi_i, qk_i, le_i, me_i = xs
        log_inter = b_i + m_st[..., None]
        m_t = jnp.maximum(log_inter, mi_i)
        w_inter = jnp.exp(log_inter - m_t)
        w_intra = jnp.exp(lw_i - m_t[..., None]) * qk_i
        num = (w_inter[..., None] * jnp.einsum('bhtk,bhkv->bhtv', q_i, c_st)
               + jnp.einsum('bhts,bhsv->bhtv', w_intra, v_i))
        den = w_inter * jnp.einsum('bhtk,bhk->bht', q_i, n_st) + jnp.sum(w_intra, axis=-1)
        h = num / jnp.maximum(jnp.abs(den), jnp.exp(-m_t))[..., None]
        b_last = b_i[..., -1]
        m_new = jnp.maximum(b_last + m_st, me_i)
        a = jnp.exp(b_last + m_st - m_new)
        wk = jnp.exp(le_i - m_new[..., None])[..., None] * k_i
        c_st = a[..., None, None] * c_st + jnp.einsum('bhsk,bhsv->bhkv', wk, v_i)
        n_st = a[..., None] * n_st + jnp.sum(wk, axis=-2)
        return (c_st, n_st, m_new), h

    init = (jnp.zeros((b_, h_, d_, d_), jnp.float32), jnp.zeros((b_, h_, d_), jnp.float32),
            jnp.zeros((b_, h_), jnp.float32))
    _, h = lax.scan(step, init, (qc, kc, vc, bcum, log_w, m_intra, qk, log_end, m_end))
    return from_chunks(h)


def mlstm(q, k, v, o_pre, i_pre, f_pre, conv_w, i_bias, f_bias, norm_w):
    b_, l_, _ = q.shape
    qk = jax.nn.silu(causal_depthwise_conv(jnp.concatenate([q, k], axis=-1), conv_w))
    q, k = jnp.split(qk.astype(jnp.float32), 2, axis=-1)
    q = q.reshape(b_, l_, MLSTM_HEADS, MLSTM_DH) * (MLSTM_DH ** -0.5)
    k = k.reshape(b_, l_, MLSTM_HEADS, MLSTM_DH)
    v = v.astype(jnp.float32).reshape(b_, l_, MLSTM_HEADS, MLSTM_DH)
    i_gate = i_pre.astype(jnp.float32) + i_bias.astype(jnp.float32)
    log_f = jax.nn.log_sigmoid(f_pre.astype(jnp.float32) + f_bias.astype(jnp.float32))
    h = mlstm_chunked(q, k, v, i_gate, log_f)
    o_gate = jax.nn.sigmoid(o_pre.astype(jnp.float32).reshape(b_, l_, MLSTM_HEADS, MLSTM_DH))
    return (o_gate * head_rms_norm(h, norm_w)).reshape(b_, l_, MLSTM_W)


def s5_mixer(u, lam_re, lam_im, log_step, b_re, b_im, c_re, c_im, d_skip, w_glu):
    b_, l_, _ = u.shape
    u32 = u.astype(jnp.float32).reshape(b_, l_, S5_GROUPS, S5_GROUP_CH)
    lr = lam_re.astype(jnp.float32)
    li = lam_im.astype(jnp.float32)
    step = jnp.exp(log_step.astype(jnp.float32))
    er = jnp.exp(lr * step)
    abar_re = er * jnp.cos(li * step)
    abar_im = er * jnp.sin(li * step)
    den = lr * lr + li * li
    coef_re = ((abar_re - 1.0) * lr + abar_im * li) / den
    coef_im = (abar_im * lr - (abar_re - 1.0) * li) / den
    br = b_re.astype(jnp.float32)
    bi = b_im.astype(jnp.float32)
    bb_re = coef_re[..., None] * br - coef_im[..., None] * bi
    bb_im = coef_re[..., None] * bi + coef_im[..., None] * br
    bu_re = jnp.einsum('blgh,gph->blgp', u32, bb_re)
    bu_im = jnp.einsum('blgh,gph->blgp', u32, bb_im)
    a_re = jnp.broadcast_to(abar_re, bu_re.shape)
    a_im = jnp.broadcast_to(abar_im, bu_im.shape)

    def combine(e1, e2):
        a1r, a1i, b1r, b1i = e1
        a2r, a2i, b2r, b2i = e2
        return (a2r * a1r - a2i * a1i, a2r * a1i + a2i * a1r,
                a2r * b1r - a2i * b1i + b2r, a2r * b1i + a2i * b1r + b2i)

    _, _, x_re, x_im = lax.associative_scan(combine, (a_re, a_im, bu_re, bu_im), axis=1)
    y = (jnp.einsum('gjp,blgp->blgj', c_re.astype(jnp.float32), x_re)
         - jnp.einsum('gjp,blgp->blgj', c_im.astype(jnp.float32), x_im))
    y = y.reshape(b_, l_, S5_W) + d_skip.astype(jnp.float32) * u32.reshape(b_, l_, S5_W)
    g = jax.nn.gelu(y, approximate=False)
    return g * jax.nn.sigmoid(g @ w_glu.astype(jnp.float32))


def conv_ffn(x, w_up, conv_w, conv_b, w_down):
    h = causal_depthwise_conv(x @ w_up, conv_w) + conv_b
    gate, up = jnp.split(h, 2, axis=-1)
    return (jax.nn.silu(gate) * up) @ w_down


def setup_inputs(seed: int = 0) -> dict:
    key = jax.random.key(seed)
    ks = jax.random.split(key, 32)
    f32 = jnp.float32

    def nrm(k, shape, scale):
        return jax.random.normal(k, shape, f32) * scale

    nl = DEPTH
    x = nrm(ks[0], (BATCH, SEQ, D_MODEL), 1.0)
    norm1_w = 1.0 + nrm(ks[1], (nl, D_MODEL), 0.02)
    w_in = nrm(ks[2], (nl, D_MODEL, D_IN), D_MODEL ** -0.5)
    gdn_conv_w = nrm(ks[3], (nl, GDN_CONV, 2 * GDN_QK + GDN_W), GDN_CONV ** -0.5)
    gdn_a_log = jnp.log(jax.random.uniform(ks[4], (nl, GDN_HEADS), f32, 1.0, 16.0))
    dt = jnp.exp(jax.random.uniform(ks[5], (nl, GDN_HEADS), f32, math.log(1e-3), math.log(1e-1)))
    gdn_dt_bias = dt + jnp.log(-jnp.expm1(-dt))
    gdn_norm_w = 1.0 + nrm(ks[6], (nl, GDN_DV), 0.02)
    mlstm_conv_w = nrm(ks[7], (nl, MLSTM_CONV, 2 * MLSTM_W), MLSTM_CONV ** -0.5)
    mlstm_i_bias = nrm(ks[8], (nl, MLSTM_HEADS), 0.1)
    mlstm_f_bias = jnp.linspace(3.0, 6.0, MLSTM_HEADS, dtype=f32)[None, :] + nrm(ks[9], (nl, MLSTM_HEADS), 0.1)
    mlstm_norm_w = 1.0 + nrm(ks[10], (nl, MLSTM_DH), 0.02)
    s5_lam_re = -0.5 + nrm(ks[11], (nl, S5_GROUPS, S5_STATE), 0.01)
    s5_lam_im = (math.pi * jnp.arange(S5_STATE, dtype=f32))[None, None, :] + nrm(ks[12], (nl, S5_GROUPS, S5_STATE), 0.01)
    s5_log_step = jax.random.uniform(ks[13], (nl, S5_GROUPS, S5_STATE), f32, math.log(1e-3), math.log(1e-1))
    s5_b_re = nrm(ks[14], (nl, S5_GROUPS, S5_STATE, S5_GROUP_CH), (2 * S5_GROUP_CH) ** -0.5)
    s5_b_im = nrm(ks[15], (nl, S5_GROUPS, S5_STATE, S5_GROUP_CH), (2 * S5_GROUP_CH) ** -0.5)
    s5_c_re = nrm(ks[16], (nl, S5_GROUPS, S5_GROUP_CH, S5_STATE), (2 * S5_STATE) ** -0.5)
    s5_c_im = nrm(ks[17], (nl, S5_GROUPS, S5_GROUP_CH, S5_STATE), (2 * S5_STATE) ** -0.5)
    s5_d = nrm(ks[18], (nl, S5_W), 1.0)
    s5_w_glu = nrm(ks[19], (nl, S5_W, S5_W), S5_W ** -0.5)
    w_out = nrm(ks[20], (nl, D_MIX, D_MODEL), D_MIX ** -0.5)
    norm2_w = 1.0 + nrm(ks[21], (nl, D_MODEL), 0.02)
    w_up = nrm(ks[22], (nl, D_MODEL, 2 * D_FF), D_MODEL ** -0.5)
    ffn_conv_w = nrm(ks[23], (nl, FFN_CONV, 2 * D_FF), FFN_CONV ** -0.5)
    ffn_conv_b = nrm(ks[24], (nl, 2 * D_FF), 0.01)
    w_down = nrm(ks[25], (nl, D_FF, D_MODEL), D_FF ** -0.5)
    final_norm_w = 1.0 + nrm(ks[26], (D_MODEL,), 0.02)
    return {'x': x, 'norm1_w': norm1_w, 'w_in': w_in, 'gdn_conv_w': gdn_conv_w,
            'gdn_a_log': gdn_a_log, 'gdn_dt_bias': gdn_dt_bias, 'gdn_norm_w': gdn_norm_w,
            'mlstm_conv_w': mlstm_conv_w, 'mlstm_i_bias': mlstm_i_bias, 'mlstm_f_bias': mlstm_f_bias,
            'mlstm_norm_w': mlstm_norm_w, 's5_lam_re': s5_lam_re, 's5_lam_im': s5_lam_im,
            's5_log_step': s5_log_step, 's5_b_re': s5_b_re, 's5_b_im': s5_b_im,
            's5_c_re': s5_c_re, 's5_c_im': s5_c_im, 's5_d': s5_d, 's5_w_glu': s5_w_glu,
            'w_out': w_out, 'norm2_w': norm2_w, 'w_up': w_up, 'ffn_conv_w': ffn_conv_w,
            'ffn_conv_b': ffn_conv_b, 'w_down': w_down, 'final_norm_w': final_norm_w}


def reference(x, norm1_w, w_in, gdn_conv_w, gdn_a_log, gdn_dt_bias, gdn_norm_w,
              mlstm_conv_w, mlstm_i_bias, mlstm_f_bias, mlstm_norm_w,
              s5_lam_re, s5_lam_im, s5_log_step, s5_b_re, s5_b_im, s5_c_re, s5_c_im,
              s5_d, s5_w_glu, w_out, norm2_w, w_up, ffn_conv_w, ffn_conv_b, w_down,
              final_norm_w):
    split_idx = [int(s) for s in np.cumsum(IN_SPLITS)[:-1]]
    for l in range(DEPTH):
        h = rms_norm(x, norm1_w[l])
        proj = h @ w_in[l]
        (g_q, g_k, g_v, g_z, g_b, g_a,
         m_q, m_k, m_v, m_o, m_i, m_f, s_u) = jnp.split(proj, split_idx, axis=-1)
        y_a = gated_deltanet(g_q, g_k, g_v, g_z, g_b, g_a, gdn_conv_w[l], gdn_a_log[l],
                             gdn_dt_bias[l], gdn_norm_w[l])
        y_b = mlstm(m_q, m_k, m_v, m_o, m_i, m_f, mlstm_conv_w[l], mlstm_i_bias[l],
                    mlstm_f_bias[l], mlstm_norm_w[l])
        y_c = s5_mixer(s_u, s5_lam_re[l], s5_lam_im[l], s5_log_step[l], s5_b_re[l], s5_b_im[l],
                       s5_c_re[l], s5_c_im[l], s5_d[l], s5_w_glu[l])
        y = jnp.concatenate([y_a, y_b, y_c], axis=-1).astype(x.dtype)
        x = x + y @ w_out[l]
        x = x + conv_ffn(rms_norm(x, norm2_w[l]), w_up[l], ffn_conv_w[l], ffn_conv_b[l], w_down[l])
    return rms_norm(x, final_norm_w)
```

```python
import functools
import math

import jax
import jax.numpy as jnp
from jax import lax
from jax.experimental import pallas as pl
from jax.experimental.pallas import tpu as pltpu

F32 = jnp.float32
BF16 = jnp.bfloat16

D_MODEL = 1024
CHUNK = 64
EPS = 1e-6
GDN_HEADS = 4
GDN_DK = 128
GDN_DV = 128
GDN_CONV = 4
GDN_QK = GDN_HEADS * GDN_DK
GDN_W = GDN_HEADS * GDN_DV
MLSTM_HEADS = 4
MLSTM_DH = 64
MLSTM_CONV = 4
MLSTM_W = MLSTM_HEADS * MLSTM_DH
S5_GROUPS = 16
S5_GROUP_CH = 16
S5_STATE = 64
S5_W = S5_GROUPS * S5_GROUP_CH
S5_NS = S5_GROUPS * S5_STATE
D_FF = 2816
FFN_CONV = 3

LANES = 128
SUBLANES = 8
GATE_W = LANES
GCOL_B, GCOL_A, GCOL_I, GCOL_F = 0, 4, 8, 12

MM_TM = 512
MIX_TL = 256
FFN_TL = 512
FFN_PIECE = 256
VMEM_LIMIT = 56 * 1024 * 1024


def _dot(a, b):
    return jnp.dot(a.astype(BF16), b.astype(BF16), preferred_element_type=F32)


def _dot_nt(a, b):
    return lax.dot_general(a.astype(BF16), b.astype(BF16), (((1,), (1,)), ((), ())),
                           preferred_element_type=F32)


def _dot_tn(a, b):
    return lax.dot_general(a.astype(BF16), b.astype(BF16), (((0,), (0,)), ((), ())),
                           preferred_element_type=F32)


def _dot_f32(a, b):
    return jnp.dot(a, b, precision=lax.Precision.HIGHEST, preferred_element_type=F32)


def _sigmoid(x):
    return 1.0 / (1.0 + jnp.exp(-x))


def _silu(x):
    return x * _sigmoid(x)


def _softplus(x):
    return jnp.maximum(x, 0.0) + jnp.log1p(jnp.exp(-jnp.abs(x)))


def _rms(x, w):
    return x * lax.rsqrt(jnp.mean(x * x, axis=-1, keepdims=True) + EPS) * w


def _chunk_cumsum(g):
    pos = lax.broadcasted_iota(jnp.int32, g.shape, 0) & (CHUNK - 1)
    k = 1
    while k < CHUNK:
        g = g + jnp.where(pos >= k, pltpu.roll(g, k, 0), 0.0)
        k *= 2
    return g


def _causal_conv(ext_ref, w_ref, cols, n_rows, k_w):
    acc = None
    for j in range(k_w):
        term = ext_ref[pl.ds(SUBLANES - (k_w - 1) + j, n_rows), cols] * w_ref[j:j + 1, cols]
        acc = term if acc is None else acc + term
    return acc


def _inproj_kernel(x_ref, nw_ref, w_ref, *out_refs):
    h = _rms(x_ref[...], nw_ref[...]).astype(BF16)
    off = 0
    for ref in out_refs:
        n = ref.shape[-1]
        ref[...] = jnp.dot(h, w_ref[:, off:off + n], preferred_element_type=F32)
        off += n


def _inproj(xf, norm_w, w_cat, widths):
    t, d = xf.shape
    n_all = w_cat.shape[1]
    const = lambda i: (0, 0)
    return pl.pallas_call(
        _inproj_kernel,
        grid=(t // MM_TM,),
        in_specs=[pl.BlockSpec((MM_TM, d), lambda i: (i, 0)),
                  pl.BlockSpec((1, d), const),
                  pl.BlockSpec((d, n_all), const, pipeline_mode=pl.Buffered(1))],
        out_specs=[pl.BlockSpec((MM_TM, n), lambda i: (i, 0)) for n in widths],
        out_shape=[jax.ShapeDtypeStruct((t, n), F32) for n in widths],
        compiler_params=pltpu.CompilerParams(dimension_semantics=("arbitrary",),
                                             vmem_limit_bytes=VMEM_LIMIT),
        name="inproj",
    )(xf, norm_w, w_cat)


def _tri_inverse(m):
    n = m.shape[0]
    eye = (lax.broadcasted_iota(jnp.int32, (n, n), 0)
           == lax.broadcasted_iota(jnp.int32, (n, n), 1)).astype(F32)
    p = eye - m
    mk = m
    k = 2
    while k < CHUNK:
        mk = _dot_f32(mk, mk)
        p = p + _dot_f32(p, mk)
        k *= 2
    return p


def _gdn_kernel(qkv_ref, z_ref, gt_ref, cw_ref, alog_ref, dtb_ref, nw_ref, y_ref,
                ext_ref, s_ref):
    tl = qkv_ref.shape[0]
    n_chunks = tl // CHUNK

    @pl.when(pl.program_id(1) == 0)
    def _():
        ext_ref[0:SUBLANES, :] = jnp.zeros((SUBLANES, ext_ref.shape[1]), F32)
        s_ref[...] = jnp.zeros(s_ref.shape, F32)

    ext_ref[SUBLANES:SUBLANES + tl, :] = qkv_ref[...]

    gates = gt_ref[...]
    beta_all = _sigmoid(gates)
    g_all = -jnp.exp(alog_ref[...]) * _softplus(gates + dtb_ref[...])
    gc = _chunk_cumsum(g_all)
    gc_t = gc.T

    row_i = lax.broadcasted_iota(jnp.int32, (tl, tl), 0)
    col_i = lax.broadcasted_iota(jnp.int32, (tl, tl), 1)
    same_chunk = (row_i // CHUNK) == (col_i // CHUNK)
    lower = same_chunk & (col_i <= row_i)
    strict = same_chunk & (col_i < row_i)

    def conv_head(slab):
        cols = slice(slab * LANES, (slab + 1) * LANES)
        return _silu(_causal_conv(ext_ref, cw_ref, cols, tl, GDN_CONV))

    for h in range(GDN_HEADS):
        q = conv_head(h)
        k = conv_head(GDN_HEADS + h)
        v = conv_head(2 * GDN_HEADS + h)
        q = q * lax.rsqrt(jnp.sum(q * q, axis=-1, keepdims=True) + EPS) * (GDN_DK ** -0.5)
        k = k * lax.rsqrt(jnp.sum(k * k, axis=-1, keepdims=True) + EPS)
        beta = beta_all[:, GCOL_B + h:GCOL_B + h + 1]
        g_col = gc[:, GCOL_A + h:GCOL_A + h + 1]
        g_row = gc_t[GCOL_A + h:GCOL_A + h + 1, :]
        decay = jnp.exp(jnp.where(lower, g_col - g_row, -jnp.inf))
        kb = k * beta
        m = jnp.where(strict, _dot_nt(kb, k) * decay, 0.0)
        t_inv = _tri_inverse(m)
        eg = jnp.exp(g_col)
        uw = _dot(t_inv, jnp.concatenate([v * beta, kb * eg], axis=1))
        u = uw[:, :GDN_DV]
        w = uw[:, GDN_DV:]
        a_qk = _dot_nt(q, k) * decay
        q_dec = q * eg

        s = s_ref[h]
        outs = []
        for c in range(n_chunks):
            r = slice(c * CHUNK, (c + 1) * CHUNK)
            g_last = g_col[(c + 1) * CHUNK - 1:(c + 1) * CHUNK, :]
            k_dec = k[r] * jnp.exp(g_last - g_col[r])
            v_new = u[r] - _dot(w[r], s)
            outs.append(_dot(q_dec[r], s) + _dot(a_qk[r, r], v_new))
            s = s * jnp.exp(g_last) + _dot_tn(k_dec, v_new)
        s_ref[h] = s

        o = jnp.concatenate(outs, axis=0)
        cols = slice(h * GDN_DV, (h + 1) * GDN_DV)
        y_ref[:, cols] = _rms(o, nw_ref[...]) * _silu(z_ref[:, cols])

    ext_ref[0:SUBLANES, :] = ext_ref[tl:tl + SUBLANES, :]


def _gdn(qkv, z, gates, conv_w, a_log_row, dt_bias_row, norm_w, batch):
    t = qkv.shape[0]
    nl = t // batch // MIX_TL
    tile = lambda b, l: (b * nl + l, 0)
    const = lambda b, l: (0, 0)
    return pl.pallas_call(
        _gdn_kernel,
        grid=(batch, nl),
        in_specs=[pl.BlockSpec((MIX_TL, 3 * GDN_QK), tile),
                  pl.BlockSpec((MIX_TL, GDN_W), tile),
                  pl.BlockSpec((MIX_TL, GATE_W), tile),
                  pl.BlockSpec((GDN_CONV, 3 * GDN_QK), const),
                  pl.BlockSpec((1, GATE_W), const),
                  pl.BlockSpec((1, GATE_W), const),
                  pl.BlockSpec((1, GDN_DV), const)],
        out_specs=pl.BlockSpec((MIX_TL, GDN_W), tile),
        out_shape=jax.ShapeDtypeStruct((t, GDN_W), F32),
        scratch_shapes=[pltpu.VMEM((MIX_TL + SUBLANES, 3 * GDN_QK), F32),
                        pltpu.VMEM((GDN_HEADS, GDN_DK, GDN_DV), F32)],
        compiler_params=pltpu.CompilerParams(dimension_semantics=("arbitrary", "arbitrary"),
                                             vmem_limit_bytes=VMEM_LIMIT),
        name="gdn",
    )(qkv, z, gates, conv_w, a_log_row, dt_bias_row, norm_w)


def _mlstm_kernel(qk_ref, v_ref, o_ref, gt_ref, cw_ref, ib_ref, fb_ref, nw_ref, y_ref,
                  ext_ref, c_ref, n_ref, m_ref):
    tl = qk_ref.shape[0]
    n_chunks = tl // CHUNK
    dh = MLSTM_DH

    @pl.when(pl.program_id(1) == 0)
    def _():
        ext_ref[0:SUBLANES, :] = jnp.zeros((SUBLANES, ext_ref.shape[1]), F32)
        c_ref[...] = jnp.zeros(c_ref.shape, F32)
        n_ref[...] = jnp.zeros(n_ref.shape, F32)
        m_ref[...] = jnp.zeros(m_ref.shape, F32)

    ext_ref[SUBLANES:SUBLANES + tl, :] = qk_ref[...]
    qk_conv = _silu(_causal_conv(ext_ref, cw_ref, slice(None), tl, MLSTM_CONV))
    ext_ref[0:SUBLANES, :] = ext_ref[tl:tl + SUBLANES, :]

    gates = gt_ref[...]
    i_all = gates + ib_ref[...]
    logf_all = -_softplus(-(gates + fb_ref[...]))
    b_all = _chunk_cumsum(logf_all)
    b_t = b_all.T
    i_t = i_all.T

    tril = (lax.broadcasted_iota(jnp.int32, (CHUNK, CHUNK), 1)
            <= lax.broadcasted_iota(jnp.int32, (CHUNK, CHUNK), 0))

    for h in range(MLSTM_HEADS):
        q = qk_conv[:, h * dh:(h + 1) * dh] * (dh ** -0.5)
        k = qk_conv[:, MLSTM_W + h * dh:MLSTM_W + (h + 1) * dh]
        v = v_ref[:, h * dh:(h + 1) * dh]
        c_st = c_ref[h]
        n_st = n_ref[h]
        m_st = m_ref[h]
        outs = []
        for c in range(n_chunks):
            r = slice(c * CHUNK, (c + 1) * CHUNK)
            b_col = b_all[r, GCOL_F + h:GCOL_F + h + 1]
            i_col = i_all[r, GCOL_I + h:GCOL_I + h + 1]
            b_row = b_t[GCOL_F + h:GCOL_F + h + 1, r]
            i_row = i_t[GCOL_I + h:GCOL_I + h + 1, r]
            b_last = b_col[CHUNK - 1:CHUNK, :]
            log_w = jnp.where(tril, b_col - b_row + i_row, -jnp.inf)
            m_intra = jnp.max(log_w, axis=-1, keepdims=True)
            qk = _dot_nt(q[r], k[r])
            m_end = jnp.max(b_last - b_row + i_row, axis=-1, keepdims=True)

            log_inter = b_col + m_st
            m_t = jnp.maximum(log_inter, m_intra)
            w_inter = jnp.exp(log_inter - m_t)
            w_intra = jnp.exp(log_w - m_t) * qk
            num = w_inter * _dot(q[r], c_st) + _dot(w_intra, v[r])
            den = (w_inter * jnp.sum(q[r] * n_st, axis=-1, keepdims=True)
                   + jnp.sum(w_intra, axis=-1, keepdims=True))
            outs.append(num / jnp.maximum(jnp.abs(den), jnp.exp(-m_t)))

            m_new = jnp.maximum(b_last + m_st, m_end)
            a = jnp.exp(b_last + m_st - m_new)
            wk = jnp.exp(b_last - b_col + i_col - m_new) * k[r]
            c_st = a * c_st + _dot_tn(wk, v[r])
            n_st = a * n_st + jnp.sum(wk, axis=0, keepdims=True)
            m_st = m_new
        c_ref[h] = c_st
        n_ref[h] = n_st
        m_ref[h] = m_st

        hh = jnp.concatenate(outs, axis=0)
        cols = slice(h * dh, (h + 1) * dh)
        y_ref[:, cols] = _sigmoid(o_ref[:, cols]) * _rms(hh, nw_ref[...])


def _mlstm(qk, v, o_pre, gates, conv_w, i_bias_row, f_bias_row, norm_w, batch):
    t = qk.shape[0]
    nl = t // batch // MIX_TL
    tile = lambda b, l: (b * nl + l, 0)
    const = lambda b, l: (0, 0)
    return pl.pallas_call(
        _mlstm_kernel,
        grid=(batch, nl),
        in_specs=[pl.BlockSpec((MIX_TL, 2 * MLSTM_W), tile),
                  pl.BlockSpec((MIX_TL, MLSTM_W), tile),
                  pl.BlockSpec((MIX_TL, MLSTM_W), tile),
                  pl.BlockSpec((MIX_TL, GATE_W), tile),
                  pl.BlockSpec((MLSTM_CONV, 2 * MLSTM_W), const),
                  pl.BlockSpec((1, GATE_W), const),
                  pl.BlockSpec((1, GATE_W), const),
                  pl.BlockSpec((1, MLSTM_DH), const)],
        out_specs=pl.BlockSpec((MIX_TL, MLSTM_W), tile),
        out_shape=jax.ShapeDtypeStruct((t, MLSTM_W), F32),
        scratch_shapes=[pltpu.VMEM((MIX_TL + SUBLANES, 2 * MLSTM_W), F32),
                        pltpu.VMEM((MLSTM_HEADS, MLSTM_DH, MLSTM_DH), F32),
                        pltpu.VMEM((MLSTM_HEADS, 1, MLSTM_DH), F32),
                        pltpu.VMEM((MLSTM_HEADS, 1, 1), F32)],
        compiler_params=pltpu.CompilerParams(dimension_semantics=("arbitrary", "arbitrary"),
                                             vmem_limit_bytes=VMEM_LIMIT),
        name="mlstm",
    )(qk, v, o_pre, gates, conv_w, i_bias_row, f_bias_row, norm_w)


def _s5_kernel(u_ref, bb_ref, apow_ref, ptab_ref, cc_ref, d_ref, wg_ref, y_ref, st_ref):
    tl = u_ref.shape[0]
    ns = S5_NS

    @pl.when(pl.program_id(1) == 0)
    def _():
        st_ref[...] = jnp.zeros(st_ref.shape, F32)

    u = u_ref[...]
    bu = _dot(u, bb_ref[...])
    xr = bu[:, :ns]
    xi = bu[:, ns:]
    row = lax.broadcasted_iota(jnp.int32, (tl, ns), 0)
    k, j = 1, 0
    while k < tl:
        keep = row >= k
        sr = jnp.where(keep, pltpu.roll(xr, k, 0), 0.0)
        si = jnp.where(keep, pltpu.roll(xi, k, 0), 0.0)
        ar = apow_ref[j:j + 1, :ns]
        ai = apow_ref[j:j + 1, ns:]
        xr, xi = xr + ar * sr - ai * si, xi + ar * si + ai * sr
        k *= 2
        j += 1
    cr = st_ref[0:1, :ns]
    ci = st_ref[0:1, ns:]
    pr = ptab_ref[:, :ns]
    pi = ptab_ref[:, ns:]
    xr, xi = xr + pr * cr - pi * ci, xi + pr * ci + pi * cr
    st_ref[0:1, :ns] = xr[tl - 1:tl, :]
    st_ref[0:1, ns:] = xi[tl - 1:tl, :]

    y = _dot(xr, cc_ref[:ns, :]) + _dot(xi, cc_ref[ns:, :]) + d_ref[...] * u
    g = 0.5 * y * (1.0 + lax.erf(y * (2.0 ** -0.5)))
    y_ref[...] = g * _sigmoid(_dot(g, wg_ref[...]))


def _s5(u, bb, apow, ptab, cc, d_row, w_glu, batch):
    t = u.shape[0]
    nl = t // batch // MIX_TL
    tile = lambda b, l: (b * nl + l, 0)
    const = lambda b, l: (0, 0)
    return pl.pallas_call(
        _s5_kernel,
        grid=(batch, nl),
        in_specs=[pl.BlockSpec((MIX_TL, S5_W), tile),
                  pl.BlockSpec(bb.shape, const),
                  pl.BlockSpec(apow.shape, const),
                  pl.BlockSpec(ptab.shape, const),
                  pl.BlockSpec(cc.shape, const),
                  pl.BlockSpec((1, S5_W), const),
                  pl.BlockSpec((S5_W, S5_W), const)],
        out_specs=pl.BlockSpec((MIX_TL, S5_W), tile),
        out_shape=jax.ShapeDtypeStruct((t, S5_W), F32),
        scratch_shapes=[pltpu.VMEM((SUBLANES, 2 * S5_NS), F32)],
        compiler_params=pltpu.CompilerParams(dimension_semantics=("arbitrary", "arbitrary"),
                                             vmem_limit_bytes=VMEM_LIMIT),
        name="s5",
    )(u, bb, apow, ptab, cc, d_row, w_glu)


def _s5_tables(lam_re, lam_im, log_step, b_re, b_im, c_re, c_im):
    step = jnp.exp(log_step)
    er = jnp.exp(lam_re * step)
    a_re = er * jnp.cos(lam_im * step)
    a_im = er * jnp.sin(lam_im * step)
    den = lam_re * lam_re + lam_im * lam_im
    coef_re = ((a_re - 1.0) * lam_re + a_im * lam_im) / den
    coef_im = (a_im * lam_re - (a_re - 1.0) * lam_im) / den
    bb_re = coef_re[..., None] * b_re - coef_im[..., None] * b_im
    bb_im = coef_re[..., None] * b_im + coef_im[..., None] * b_re
    eye = jnp.eye(S5_GROUPS, dtype=F32)
    blk_in = lambda m: jnp.einsum('gph,gk->ghkp', m, eye).reshape(S5_W, S5_NS)
    blk_out = lambda m: jnp.einsum('gjp,gk->gpkj', m, eye).reshape(S5_NS, S5_W)
    bb = jnp.concatenate([blk_in(bb_re), blk_in(bb_im)], axis=1).astype(BF16)
    cc = jnp.concatenate([blk_out(c_re), -blk_out(c_im)], axis=0).astype(BF16)

    ar = a_re.reshape(1, S5_NS)
    ai = a_im.reshape(1, S5_NS)
    pows_r, pows_i = [ar], [ai]
    k = 2
    while k < MIX_TL:
        r, i = pows_r[-1], pows_i[-1]
        pows_r.append(r * r - i * i)
        pows_i.append(2.0 * r * i)
        k *= 2
    tab_r, tab_i = ar, ai
    for r, i in zip(pows_r, pows_i):
        tab_r, tab_i = (jnp.concatenate([tab_r, tab_r * r - tab_i * i], axis=0),
                        jnp.concatenate([tab_i, tab_r * i + tab_i * r], axis=0))
    apow = jnp.concatenate([jnp.concatenate(pows_r, axis=0), jnp.concatenate(pows_i, axis=0)], axis=1)
    ptab = jnp.concatenate([tab_r, tab_i], axis=1)
    return bb, apow, ptab, cc


def _ffn_kernel(x_ref, ya_ref, yb_ref, yc_ref, wo_ref, n2_ref, wup_ref, cw_ref, cb_ref,
                wdn_ref, fn_ref, out_ref, acc_ref, h_ref, act_ref, ext_ref, carry_ref, *, final_norm):
    tl = x_ref.shape[0]

    @pl.when(pl.program_id(1) == 0)
    def _():
        carry_ref[...] = jnp.zeros(carry_ref.shape, F32)

    x_mid = (x_ref[...]
             + _dot(ya_ref[...], wo_ref[0:GDN_W, :])
             + _dot(yb_ref[...], wo_ref[GDN_W:GDN_W + MLSTM_W, :])
             + _dot(yc_ref[...], wo_ref[GDN_W + MLSTM_W:, :]))
    acc_ref[...] = x_mid
    h_ref[...] = _rms(x_mid, n2_ref[...]).astype(BF16)

    def branch(cols):
        ext_ref[0:SUBLANES, :] = carry_ref[:, cols]
        ext_ref[SUBLANES:SUBLANES + tl, :] = jnp.dot(h_ref[...], wup_ref[:, cols],
                                                     preferred_element_type=F32)
        carry_ref[:, cols] = ext_ref[tl:tl + SUBLANES, :]
        acc = None
        for j in range(FFN_CONV):
            term = (ext_ref[pl.ds(SUBLANES - (FFN_CONV - 1) + j, tl), :]
                    * cw_ref[j:j + 1, cols])
            acc = term if acc is None else acc + term
        return acc + cb_ref[:, cols]

    for p in range(D_FF // FFN_PIECE):
        lo = p * FFN_PIECE
        gate = branch(slice(lo, lo + FFN_PIECE))
        up = branch(slice(D_FF + lo, D_FF + lo + FFN_PIECE))
        act_ref[:, lo:lo + FFN_PIECE] = (_silu(gate) * up).astype(BF16)

    out = acc_ref[...] + jnp.dot(act_ref[...], wdn_ref[...], preferred_element_type=F32)
    if final_norm:
        out = _rms(out, fn_ref[...])
    out_ref[...] = out


def _ffn(xf, ya, yb, yc, w_out, norm2_w, w_up, conv_w, conv_b, w_down, final_w, batch, final_norm):
    t, d = xf.shape
    nl = t // batch // FFN_TL
    tile = lambda b, l: (b * nl + l, 0)
    const = lambda b, l: (0, 0)
    resident = functools.partial(pl.BlockSpec, index_map=const, pipeline_mode=pl.Buffered(1))
    return pl.pallas_call(
        functools.partial(_ffn_kernel, final_norm=final_norm),
        grid=(batch, nl),
        in_specs=[pl.BlockSpec((FFN_TL, d), tile),
                  pl.BlockSpec((FFN_TL, GDN_W), tile),
                  pl.BlockSpec((FFN_TL, MLSTM_W), tile),
                  pl.BlockSpec((FFN_TL, S5_W), tile),
                  resident(w_out.shape),
                  pl.BlockSpec((1, d), const),
                  resident(w_up.shape),
                  pl.BlockSpec(conv_w.shape, const),
                  pl.BlockSpec(conv_b.shape, const),
                  resident(w_down.shape),
                  pl.BlockSpec((1, d), const)],
        out_specs=pl.BlockSpec((FFN_TL, d), tile),
        out_shape=jax.ShapeDtypeStruct((t, d), F32),
        scratch_shapes=[pltpu.VMEM((FFN_TL, d), F32),
                        pltpu.VMEM((FFN_TL, d), BF16),
                        pltpu.VMEM((FFN_TL, D_FF), BF16),
                        pltpu.VMEM((FFN_TL + SUBLANES, FFN_PIECE), F32),
                        pltpu.VMEM((SUBLANES, 2 * D_FF), F32)],
        compiler_params=pltpu.CompilerParams(dimension_semantics=("arbitrary", "arbitrary"),
                                             vmem_limit_bytes=VMEM_LIMIT),
        name="ffn",
    )(xf, ya, yb, yc, w_out, norm2_w, w_up, conv_w, conv_b, w_down, final_w)


def _gate_row(vals, col):
    return jnp.zeros((1, GATE_W), F32).at[0, col:col + vals.shape[0]].set(vals.astype(F32))


def _layer(xf, batch, final_norm, norm1_w, w_in, gdn_conv_w, gdn_a_log, gdn_dt_bias, gdn_norm_w,
           mlstm_conv_w, mlstm_i_bias, mlstm_f_bias, mlstm_norm_w,
           s5_lam_re, s5_lam_im, s5_log_step, s5_b_re, s5_b_im, s5_c_re, s5_c_im,
           s5_d, s5_w_glu, w_out, norm2_w, w_up, ffn_conv_w, ffn_conv_b, w_down, final_norm_w):
    d = xf.shape[1]
    splits = (GDN_QK, GDN_QK, GDN_W, GDN_W, GDN_HEADS, GDN_HEADS,
              MLSTM_W, MLSTM_W, MLSTM_W, MLSTM_W, MLSTM_HEADS, MLSTM_HEADS, S5_W)
    offs = [0]
    for s in splits:
        offs.append(offs[-1] + s)
    seg = lambda i, j: w_in[:, offs[i]:offs[j]]
    gate_cols = jnp.concatenate(
        [seg(4, 6), seg(10, 12), jnp.zeros((d, GATE_W - 2 * GDN_HEADS - 2 * MLSTM_HEADS), F32)], axis=1)
    w_cat = jnp.concatenate([seg(0, 3), seg(3, 4), seg(6, 8), seg(8, 9), seg(9, 10), seg(12, 13),
                             gate_cols], axis=1).astype(BF16)
    widths = (3 * GDN_QK, GDN_W, 2 * MLSTM_W, MLSTM_W, MLSTM_W, S5_W, GATE_W)
    qkv, z, mqk, mv, mo, su, gates = _inproj(xf, norm1_w.reshape(1, d), w_cat, widths)

    ya = _gdn(qkv, z, gates, gdn_conv_w, _gate_row(gdn_a_log, GCOL_A), _gate_row(gdn_dt_bias, GCOL_A),
              gdn_norm_w.reshape(1, GDN_DV), batch)
    yb = _mlstm(mqk, mv, mo, gates, mlstm_conv_w, _gate_row(mlstm_i_bias, GCOL_I),
                _gate_row(mlstm_f_bias, GCOL_F), mlstm_norm_w.reshape(1, MLSTM_DH), batch)
    bb, apow, ptab, cc = _s5_tables(s5_lam_re, s5_lam_im, s5_log_step, s5_b_re, s5_b_im, s5_c_re, s5_c_im)
    yc = _s5(su, bb, apow, ptab, cc, s5_d.reshape(1, S5_W), s5_w_glu.astype(BF16), batch)

    return _ffn(xf, ya, yb, yc, w_out.astype(BF16), norm2_w.reshape(1, d), w_up.astype(BF16),
                ffn_conv_w, ffn_conv_b.reshape(1, 2 * D_FF), w_down.astype(BF16),
                final_norm_w.reshape(1, d), batch, final_norm)


def kernel(x, norm1_w, w_in, gdn_conv_w, gdn_a_log, gdn_dt_bias, gdn_norm_w, mlstm_conv_w, mlstm_i_bias, mlstm_f_bias, mlstm_norm_w, s5_lam_re, s5_lam_im, s5_log_step, s5_b_re, s5_b_im, s5_c_re, s5_c_im, s5_d, s5_w_glu, w_out, norm2_w, w_up, ffn_conv_w, ffn_conv_b, w_down, final_norm_w):
    batch, seq, d = x.shape
    depth = w_in.shape[0]
    assert seq % FFN_TL == 0 and seq % MIX_TL == 0 and (batch * seq) % MM_TM == 0
    xf = x.reshape(batch * seq, d)
    per_layer = (norm1_w, w_in, gdn_conv_w, gdn_a_log, gdn_dt_bias, gdn_norm_w, mlstm_conv_w,
                 mlstm_i_bias, mlstm_f_bias, mlstm_norm_w, s5_lam_re, s5_lam_im, s5_log_step,
                 s5_b_re, s5_b_im, s5_c_re, s5_c_im, s5_d, s5_w_glu, w_out, norm2_w, w_up,
                 ffn_conv_w, ffn_conv_b, w_down)
    for l in range(depth):
        xf = _layer(xf, batch, l == depth - 1, *(p[l] for p in per_layer), final_norm_w)
    return xf.reshape(batch, seq, d)
```

```python
import functools
import math

import jax
import jax.numpy as jnp
from jax import lax
from jax.experimental import pallas as pl
from jax.experimental.pallas import tpu as pltpu

F32 = jnp.float32
BF16 = jnp.bfloat16

D_MODEL = 1024
CHUNK = 64
EPS = 1e-6
GDN_HEADS = 4
GDN_DK = 128
GDN_DV = 128
GDN_CONV = 4
GDN_QK = GDN_HEADS * GDN_DK
GDN_W = GDN_HEADS * GDN_DV
MLSTM_HEADS = 4
MLSTM_DH = 64
MLSTM_CONV = 4
MLSTM_W = MLSTM_HEADS * MLSTM_DH
S5_GROUPS = 16
S5_GROUP_CH = 16
S5_STATE = 64
S5_W = S5_GROUPS * S5_GROUP_CH
S5_NS = S5_GROUPS * S5_STATE
D_FF = 2816
FFN_CONV = 3

LANES = 128
SUBLANES = 8
GATE_W = LANES
GCOL_B, GCOL_A, GCOL_I, GCOL_F = 0, 4, 8, 12

MM_TM = 512
MIX_TL = 256
FFN_TL = 512
FFN_PIECE = 256
VMEM_LIMIT = 56 * 1024 * 1024


def _dot(a, b):
    return jnp.dot(a.astype(BF16), b.astype(BF16), preferred_element_type=F32)


def _dot_nt(a, b):
    return lax.dot_general(a.astype(BF16), b.astype(BF16), (((1,), (1,)), ((), ())),
                           preferred_element_type=F32)


def _dot_tn(a, b):
    return lax.dot_general(a.astype(BF16), b.astype(BF16), (((0,), (0,)), ((), ())),
                           preferred_element_type=F32)


def _sigmoid(x):
    return 1.0 / (1.0 + jnp.exp(-x))


def _silu(x):
    return x * _sigmoid(x)


def _softplus(x):
    return jnp.maximum(x, 0.0) + jnp.log1p(jnp.exp(-jnp.abs(x)))


def _rms(x, w):
    return x * lax.rsqrt(jnp.mean(x * x, axis=-1, keepdims=True) + EPS) * w


def _cumsum_rows(g):
    pos = lax.broadcasted_iota(jnp.int32, g.shape, 0)
    k = 1
    while k < g.shape[0]:
        g = g + jnp.where(pos >= k, pltpu.roll(g, k, 0), 0.0)
        k *= 2
    return g


def _causal_conv(ext_ref, w_ref, cols, n_rows, k_w):
    acc = None
    for j in range(k_w):
        term = ext_ref[pl.ds(SUBLANES - (k_w - 1) + j, n_rows), cols] * w_ref[j:j + 1, cols]
        acc = term if acc is None else acc + term
    return acc


def _inproj_kernel(x_ref, nw_ref, w_ref, *out_refs):
    h = _rms(x_ref[...], nw_ref[...]).astype(BF16)
    off = 0
    for ref in out_refs:
        n = ref.shape[-1]
        ref[...] = jnp.dot(h, w_ref[:, off:off + n], preferred_element_type=F32)
        off += n


def _inproj(xf, norm_w, w_cat, widths):
    t, d = xf.shape
    n_all = w_cat.shape[1]
    const = lambda i: (0, 0)
    return pl.pallas_call(
        _inproj_kernel,
        grid=(t // MM_TM,),
        in_specs=[pl.BlockSpec((MM_TM, d), lambda i: (i, 0)),
                  pl.BlockSpec((1, d), const),
                  pl.BlockSpec((d, n_all), const, pipeline_mode=pl.Buffered(1))],
        out_specs=[pl.BlockSpec((MM_TM, n), lambda i: (i, 0)) for n in widths],
        out_shape=[jax.ShapeDtypeStruct((t, n), F32) for n in widths],
        compiler_params=pltpu.CompilerParams(dimension_semantics=("arbitrary",),
                                             vmem_limit_bytes=VMEM_LIMIT),
        name="inproj",
    )(xf, norm_w, w_cat)


TRI_BASE = 16


def _tri_inverse(ms):
    n = ms[0].shape[0]
    row_i = lax.broadcasted_iota(jnp.int32, (n, n), 0)
    col_i = lax.broadcasted_iota(jnp.int32, (n, n), 1)

    def same_block(size):
        return (row_i // size) == (col_i // size)

    eye = (row_i == col_i).astype(F32)
    diag = same_block(TRI_BASE)
    mks = [jnp.where(diag, m, 0.0) for m in ms]
    xs = [eye - mk for mk in mks]
    k = 2
    while k < TRI_BASE:
        mks = [_dot(mk, mk) for mk in mks]
        xs = [x + _dot(x, mk) for x, mk in zip(xs, mks)]
        k *= 2
    size = TRI_BASE
    while size < n:
        off = same_block(2 * size) & jnp.logical_not(same_block(size))
        cxs = [_dot(jnp.where(off, m, 0.0), x) for m, x in zip(ms, xs)]
        xs = [x - _dot(x, cx) for x, cx in zip(xs, cxs)]
        size *= 2
    return xs


def _gdn_kernel(qkv_ref, z_ref, gt_ref, cw_ref, alog_ref, dtb_ref, nw_ref, y_ref,
                ext_ref, s_ref):
    tl = qkv_ref.shape[0]

    @pl.when(pl.program_id(1) == 0)
    def _():
        ext_ref[0:SUBLANES, :] = jnp.zeros((SUBLANES, ext_ref.shape[1]), F32)
        s_ref[...] = jnp.zeros(s_ref.shape, F32)

    ext_ref[SUBLANES:SUBLANES + tl, :] = qkv_ref[...]

    gates = gt_ref[...]
    beta_all = _sigmoid(gates)
    g_all = -jnp.exp(alog_ref[...]) * _softplus(gates + dtb_ref[...])
    gc = _cumsum_rows(g_all)
    gc_t = gc.T

    row_i = lax.broadcasted_iota(jnp.int32, (tl, tl), 0)
    col_i = lax.broadcasted_iota(jnp.int32, (tl, tl), 1)
    lower = col_i <= row_i
    strict = col_i < row_i

    def conv_head(slab):
        cols = slice(slab * LANES, (slab + 1) * LANES)
        return _silu(_causal_conv(ext_ref, cw_ref, cols, tl, GDN_CONV))

    ms, rhss, a_qks, q_decs, k_decs, g_lasts = [], [], [], [], [], []
    for h in range(GDN_HEADS):
        q = conv_head(h)
        k = conv_head(GDN_HEADS + h)
        v = conv_head(2 * GDN_HEADS + h)
        q = q * lax.rsqrt(jnp.sum(q * q, axis=-1, keepdims=True) + EPS) * (GDN_DK ** -0.5)
        k = k * lax.rsqrt(jnp.sum(k * k, axis=-1, keepdims=True) + EPS)
        beta = beta_all[:, GCOL_B + h:GCOL_B + h + 1]
        g_col = gc[:, GCOL_A + h:GCOL_A + h + 1]
        g_row = gc_t[GCOL_A + h:GCOL_A + h + 1, :]
        g_last = g_col[tl - 1:tl, :]
        decay = jnp.exp(jnp.where(lower, g_col - g_row, -jnp.inf))
        kb = k * beta
        eg = jnp.exp(g_col)
        ms.append(jnp.where(strict, _dot_nt(kb, k) * decay, 0.0))
        rhss.append(jnp.concatenate([v * beta, kb * eg], axis=1).astype(BF16))
        a_qks.append((_dot_nt(q, k) * decay).astype(BF16))
        q_decs.append((q * eg).astype(BF16))
        k_decs.append((k * jnp.exp(g_last - g_col)).astype(BF16))
        g_lasts.append(g_last)
    ext_ref[0:SUBLANES, :] = ext_ref[tl:tl + SUBLANES, :]

    t_invs = _tri_inverse(ms)

    for h in range(GDN_HEADS):
        uw = _dot(t_invs[h], rhss[h])
        s = s_ref[h]
        v_new = uw[:, :GDN_DV] - _dot(uw[:, GDN_DV:], s)
        o = _dot(q_decs[h], s) + _dot(a_qks[h], v_new)
        s_ref[h] = s * jnp.exp(g_lasts[h]) + _dot_tn(k_decs[h], v_new)
        cols = slice(h * GDN_DV, (h + 1) * GDN_DV)
        y_ref[:, cols] = _rms(o, nw_ref[...]) * _silu(z_ref[:, cols])


def _gdn(qkv, z, gates, conv_w, a_log_row, dt_bias_row, norm_w, batch):
    t = qkv.shape[0]
    nl = t // batch // MIX_TL
    tile = lambda b, l: (b * nl + l, 0)
    const = lambda b, l: (0, 0)
    return pl.pallas_call(
        _gdn_kernel,
        grid=(batch, nl),
        in_specs=[pl.BlockSpec((MIX_TL, 3 * GDN_QK), tile),
                  pl.BlockSpec((MIX_TL, GDN_W), tile),
                  pl.BlockSpec((MIX_TL, GATE_W), tile),
                  pl.BlockSpec((GDN_CONV, 3 * GDN_QK), const),
                  pl.BlockSpec((1, GATE_W), const),
                  pl.BlockSpec((1, GATE_W), const),
                  pl.BlockSpec((1, GDN_DV), const)],
        out_specs=pl.BlockSpec((MIX_TL, GDN_W), tile),
        out_shape=jax.ShapeDtypeStruct((t, GDN_W), F32),
        scratch_shapes=[pltpu.VMEM((MIX_TL + SUBLANES, 3 * GDN_QK), F32),
                        pltpu.VMEM((GDN_HEADS, GDN_DK, GDN_DV), F32)],
        compiler_params=pltpu.CompilerParams(dimension_semantics=("arbitrary", "arbitrary"),
                                             vmem_limit_bytes=VMEM_LIMIT),
        name="gdn",
    )(qkv, z, gates, conv_w, a_log_row, dt_bias_row, norm_w)


def _mlstm_kernel(qk_ref, v_ref, o_ref, gt_ref, cw_ref, ib_ref, fb_ref, nw_ref, y_ref,
                  ext_ref, c_ref, n_ref, m_ref):
    tl = qk_ref.shape[0]
    dh = MLSTM_DH
    width = MLSTM_W

    @pl.when(pl.program_id(1) == 0)
    def _():
        ext_ref[0:SUBLANES, :] = jnp.zeros((SUBLANES, ext_ref.shape[1]), F32)
        c_ref[...] = jnp.zeros(c_ref.shape, F32)
        n_ref[...] = jnp.zeros(n_ref.shape, F32)
        m_ref[...] = jnp.zeros(m_ref.shape, F32)

    ext_ref[SUBLANES:SUBLANES + tl, :] = qk_ref[...]
    qk_conv = _silu(_causal_conv(ext_ref, cw_ref, slice(None), tl, MLSTM_CONV))
    ext_ref[0:SUBLANES, :] = ext_ref[tl:tl + SUBLANES, :]
    q = qk_conv[:, :width] * (dh ** -0.5)
    k = qk_conv[:, width:]
    v = v_ref[...]
    q_bf = q.astype(BF16)
    k_bf = k.astype(BF16)
    v_bf = v.astype(BF16)

    gates = gt_ref[...]
    i_all = gates + ib_ref[...]
    logf_all = -_softplus(-(gates + fb_ref[...]))
    b_all = _cumsum_rows(logf_all)
    b_t = b_all.T
    i_t = i_all.T

    tril = (lax.broadcasted_iota(jnp.int32, (tl, tl), 1)
            <= lax.broadcasted_iota(jnp.int32, (tl, tl), 0))
    lane_head = lax.broadcasted_iota(jnp.int32, (1, width), 1) // dh
    c_st = c_ref[...]
    n_st = n_ref[...]
    q_c = _dot(q_bf, c_st)
    q_n = q * n_st

    hh = jnp.zeros((tl, width), F32)
    wk_scale = jnp.zeros((tl, width), F32)
    a_lanes = jnp.zeros((1, width), F32)
    for h in range(MLSTM_HEADS):
        in_head = lane_head == h
        b_col = b_all[:, GCOL_F + h:GCOL_F + h + 1]
        i_col = i_all[:, GCOL_I + h:GCOL_I + h + 1]
        b_row = b_t[GCOL_F + h:GCOL_F + h + 1, :]
        i_row = i_t[GCOL_I + h:GCOL_I + h + 1, :]
        b_last = b_col[tl - 1:tl, :]
        m_st = m_ref[h]
        src = i_row - b_row
        log_w = jnp.where(tril, b_col + src, -jnp.inf)
        m_intra = jnp.max(log_w, axis=-1, keepdims=True)
        log_inter = b_col + m_st
        m_t = jnp.maximum(log_inter, m_intra)
        w_inter = jnp.exp(log_inter - m_t)
        qk = _dot_nt(jnp.where(in_head, q_bf, jnp.zeros_like(q_bf)), k_bf)
        w_intra = jnp.exp(log_w - m_t) * qk
        num = w_inter * q_c + _dot(w_intra, v_bf)
        den = (w_inter * jnp.sum(jnp.where(in_head, q_n, 0.0), axis=-1, keepdims=True)
               + jnp.sum(w_intra, axis=-1, keepdims=True))
        out = num / jnp.maximum(jnp.abs(den), jnp.exp(-m_t))
        hh = jnp.where(in_head, out, hh)

        m_new = jnp.maximum(b_last + m_st, jnp.max(b_last + src, axis=-1, keepdims=True))
        a_lanes = jnp.where(in_head, jnp.exp(b_last + m_st - m_new), a_lanes)
        wk_scale = jnp.where(in_head, jnp.exp(b_last - b_col + i_col - m_new), wk_scale)
        m_ref[h] = m_new

    wk = wk_scale * k
    row_head = lax.broadcasted_iota(jnp.int32, (width, 1), 0) // dh
    c_ref[...] = a_lanes * c_st + jnp.where(row_head == lane_head, _dot_tn(wk, v_bf), 0.0)
    n_ref[...] = a_lanes * n_st + jnp.sum(wk, axis=0, keepdims=True)

    sq = hh * hh
    rs = jnp.zeros((tl, width), F32)
    for h in range(MLSTM_HEADS):
        in_head = lane_head == h
        ms = jnp.sum(jnp.where(in_head, sq, 0.0), axis=-1, keepdims=True) * (1.0 / dh)
        rs = jnp.where(in_head, lax.rsqrt(ms + EPS), rs)
    y_ref[...] = _sigmoid(o_ref[...]) * (hh * rs * nw_ref[...])


def _mlstm(qk, v, o_pre, gates, conv_w, i_bias_row, f_bias_row, norm_w, batch):
    t = qk.shape[0]
    nl = t // batch // MIX_TL
    tile = lambda b, l: (b * nl + l, 0)
    const = lambda b, l: (0, 0)
    return pl.pallas_call(
        _mlstm_kernel,
        grid=(batch, nl),
        in_specs=[pl.BlockSpec((MIX_TL, 2 * MLSTM_W), tile),
                  pl.BlockSpec((MIX_TL, MLSTM_W), tile),
                  pl.BlockSpec((MIX_TL, MLSTM_W), tile),
                  pl.BlockSpec((MIX_TL, GATE_W), tile),
                  pl.BlockSpec((MLSTM_CONV, 2 * MLSTM_W), const),
                  pl.BlockSpec((1, GATE_W), const),
                  pl.BlockSpec((1, GATE_W), const),
                  pl.BlockSpec((1, MLSTM_W), const)],
        out_specs=pl.BlockSpec((MIX_TL, MLSTM_W), tile),
        out_shape=jax.ShapeDtypeStruct((t, MLSTM_W), F32),
        scratch_shapes=[pltpu.VMEM((MIX_TL + SUBLANES, 2 * MLSTM_W), F32),
                        pltpu.VMEM((MLSTM_W, MLSTM_W), F32),
                        pltpu.VMEM((1, MLSTM_W), F32),
                        pltpu.VMEM((MLSTM_HEADS, 1, 1), F32)],
        compiler_params=pltpu.CompilerParams(dimension_semantics=("arbitrary", "arbitrary"),
                                             vmem_limit_bytes=VMEM_LIMIT),
        name="mlstm",
    )(qk, v, o_pre, gates, conv_w, i_bias_row, f_bias_row, norm_w)


S5_MINI = 4
S5_GROUP = S5_MINI * SUBLANES


def _cmul_add(ar, ai, xr, xi, br, bi):
    return ar * xr - ai * xi + br, ar * xi + ai * xr + bi


def _s5_kernel(u_ref, bb_ref, pw_ref, pm_ref, cc_ref, d_ref, wg_ref, y_ref,
               st_ref, bu_ref, xb_ref):
    tl = u_ref.shape[0]
    ns = S5_NS

    @pl.when(pl.program_id(1) == 0)
    def _():
        st_ref[...] = jnp.zeros(st_ref.shape, F32)

    u = u_ref[...]
    bu = _dot(u, bb_ref[...])
    n_lt = ns // LANES
    for c in range(2 * n_lt):
        bu_ref[c] = bu[:, c * LANES:(c + 1) * LANES]

    sub = lax.broadcasted_iota(jnp.int32, (SUBLANES, LANES), 0)
    for c in range(n_lt):
        re_c, im_c = c, n_lt + c
        lanes_r = slice(c * LANES, (c + 1) * LANES)
        lanes_i = slice(ns + c * LANES, ns + (c + 1) * LANES)
        bcast = lambda row: jnp.broadcast_to(row, (SUBLANES, LANES))
        a_r = [bcast(pw_ref[j:j + 1, lanes_r]) for j in range(S5_MINI)]
        a_i = [bcast(pw_ref[j:j + 1, lanes_i]) for j in range(S5_MINI)]
        gin_r = st_ref[0:1, lanes_r]
        gin_i = st_ref[0:1, lanes_i]
        for g in range(tl // S5_GROUP):
            base = g * S5_GROUP
            rows = [pl.ds(base + j, SUBLANES, stride=S5_MINI) for j in range(S5_MINI)]
            xr = [bu_ref[re_c, rows[0], :]]
            xi = [bu_ref[im_c, rows[0], :]]
            for j in range(1, S5_MINI):
                nr, ni = _cmul_add(a_r[0], a_i[0], xr[-1], xi[-1],
                                   bu_ref[re_c, rows[j], :], bu_ref[im_c, rows[j], :])
                xr.append(nr)
                xi.append(ni)
            er, ei = xr[-1], xi[-1]
            k, j = 1, S5_MINI - 1
            while k < SUBLANES:
                keep = sub >= k
                sr = jnp.where(keep, pltpu.roll(er, k, 0), 0.0)
                si = jnp.where(keep, pltpu.roll(ei, k, 0), 0.0)
                er, ei = _cmul_add(pw_ref[j:j + 1, lanes_r], pw_ref[j:j + 1, lanes_i], sr, si, er, ei)
                k *= 2
                j += 1
            er, ei = _cmul_add(pm_ref[:, lanes_r], pm_ref[:, lanes_i], gin_r, gin_i, er, ei)
            pr = jnp.where(sub == 0, gin_r, pltpu.roll(er, 1, 0))
            pi = jnp.where(sub == 0, gin_i, pltpu.roll(ei, 1, 0))
            gin_r = er[SUBLANES - 1:SUBLANES, :]
            gin_i = ei[SUBLANES - 1:SUBLANES, :]
            for j in range(S5_MINI):
                fr, fi = _cmul_add(a_r[j], a_i[j], pr, pi, xr[j], xi[j])
                bu_ref[re_c, rows[j], :] = fr
                bu_ref[im_c, rows[j], :] = fi
        st_ref[0:1, lanes_r] = gin_r
        st_ref[0:1, lanes_i] = gin_i

    for c in range(2 * n_lt):
        xb_ref[:, c * LANES:(c + 1) * LANES] = bu_ref[c].astype(BF16)

    y = jnp.dot(xb_ref[...], cc_ref[...], preferred_element_type=F32) + d_ref[...] * u
    g = 0.5 * y * (1.0 + lax.erf(y * (2.0 ** -0.5)))
    y_ref[...] = g * _sigmoid(_dot(g, wg_ref[...]))


def _s5(u, bb, pw, pm, cc, d_row, w_glu, batch):
    t = u.shape[0]
    nl = t // batch // MIX_TL
    tile = lambda b, l: (b * nl + l, 0)
    const = lambda b, l: (0, 0)
    return pl.pallas_call(
        _s5_kernel,
        grid=(batch, nl),
        in_specs=[pl.BlockSpec((MIX_TL, S5_W), tile),
                  pl.BlockSpec(bb.shape, const),
                  pl.BlockSpec(pw.shape, const),
                  pl.BlockSpec(pm.shape, const),
                  pl.BlockSpec(cc.shape, const),
                  pl.BlockSpec((1, S5_W), const),
                  pl.BlockSpec((S5_W, S5_W), const)],
        out_specs=pl.BlockSpec((MIX_TL, S5_W), tile),
        out_shape=jax.ShapeDtypeStruct((t, S5_W), F32),
        scratch_shapes=[pltpu.VMEM((SUBLANES, 2 * S5_NS), F32),
                        pltpu.VMEM((2 * S5_NS // LANES, MIX_TL, LANES), F32),
                        pltpu.VMEM((MIX_TL, 2 * S5_NS), BF16)],
        compiler_params=pltpu.CompilerParams(dimension_semantics=("arbitrary", "arbitrary"),
                                             vmem_limit_bytes=VMEM_LIMIT),
        name="s5",
    )(u, bb, pw, pm, cc, d_row, w_glu)


def _s5_tables(lam_re, lam_im, log_step, b_re, b_im, c_re, c_im):
    step = jnp.exp(log_step)
    er = jnp.exp(lam_re * step)
    a_re = er * jnp.cos(lam_im * step)
    a_im = er * jnp.sin(lam_im * step)
    den = lam_re * lam_re + lam_im * lam_im
    coef_re = ((a_re - 1.0) * lam_re + a_im * lam_im) / den
    coef_im = (a_im * lam_re - (a_re - 1.0) * lam_im) / den
    bb_re = coef_re[..., None] * b_re - coef_im[..., None] * b_im
    bb_im = coef_re[..., None] * b_im + coef_im[..., None] * b_re
    eye = jnp.eye(S5_GROUPS, dtype=F32)
    blk_in = lambda m: jnp.einsum('gph,gk->ghkp', m, eye).reshape(S5_W, S5_NS)
    blk_out = lambda m: jnp.einsum('gjp,gk->gpkj', m, eye).reshape(S5_NS, S5_W)
    bb = jnp.concatenate([blk_in(bb_re), blk_in(bb_im)], axis=1).astype(BF16)
    cc = jnp.concatenate([blk_out(c_re), -blk_out(c_im)], axis=0).astype(BF16)

    ar = a_re.reshape(1, S5_NS)
    ai = a_im.reshape(1, S5_NS)
    tab_r, tab_i = ar, ai
    while tab_r.shape[0] < S5_GROUP:
        r, i = tab_r[-1:], tab_i[-1:]
        tab_r, tab_i = (jnp.concatenate([tab_r, tab_r * r - tab_i * i], axis=0),
                        jnp.concatenate([tab_i, tab_r * i + tab_i * r], axis=0))
    ptab = jnp.concatenate([tab_r, tab_i], axis=1)
    steps = list(range(S5_MINI)) + [S5_MINI * (2 << j) - 1 for j in range(int(math.log2(SUBLANES)) - 1)]
    pw = jnp.concatenate([ptab[jnp.array(steps)],
                          jnp.zeros((SUBLANES - len(steps), 2 * S5_NS), F32)], axis=0)
    pm = ptab[S5_MINI - 1::S5_MINI]
    return bb, pw, pm, cc


def _ffn_kernel(x_ref, ya_ref, yb_ref, yc_ref, wo_ref, n2_ref, wup_ref, cw_ref, cb_ref,
                wdn_ref, fn_ref, out_ref, acc_ref, h_ref, act_ref, ext_ref, carry_ref, *, final_norm):
    tl = x_ref.shape[0]

    @pl.when(pl.program_id(1) == 0)
    def _():
        carry_ref[...] = jnp.zeros(carry_ref.shape, F32)

    x_mid = (x_ref[...]
             + _dot(ya_ref[...], wo_ref[0:GDN_W, :])
             + _dot(yb_ref[...], wo_ref[GDN_W:GDN_W + MLSTM_W, :])
             + _dot(yc_ref[...], wo_ref[GDN_W + MLSTM_W:, :]))
    acc_ref[...] = x_mid
    h_ref[...] = _rms(x_mid, n2_ref[...]).astype(BF16)

    def branch(cols):
        ext_ref[0:SUBLANES, :] = carry_ref[:, cols]
        ext_ref[SUBLANES:SUBLANES + tl, :] = jnp.dot(h_ref[...], wup_ref[:, cols],
                                                     preferred_element_type=F32)
        carry_ref[:, cols] = ext_ref[tl:tl + SUBLANES, :]
        acc = None
        for j in range(FFN_CONV):
            term = (ext_ref[pl.ds(SUBLANES - (FFN_CONV - 1) + j, tl), :]
                    * cw_ref[j:j + 1, cols])
            acc = term if acc is None else acc + term
        return acc + cb_ref[:, cols]

    for p in range(D_FF // FFN_PIECE):
        lo = p * FFN_PIECE
        gate = branch(slice(lo, lo + FFN_PIECE))
        up = branch(slice(D_FF + lo, D_FF + lo + FFN_PIECE))
        act_ref[:, lo:lo + FFN_PIECE] = (_silu(gate) * up).astype(BF16)

    out = acc_ref[...] + jnp.dot(act_ref[...], wdn_ref[...], preferred_element_type=F32)
    if final_norm:
        out = _rms(out, fn_ref[...])
    out_ref[...] = out


def _ffn(xf, ya, yb, yc, w_out, norm2_w, w_up, conv_w, conv_b, w_down, final_w, batch, final_norm):
    t, d = xf.shape
    nl = t // batch // FFN_TL
    tile = lambda b, l: (b * nl + l, 0)
    const = lambda b, l: (0, 0)
    resident = functools.partial(pl.BlockSpec, index_map=const, pipeline_mode=pl.Buffered(1))
    return pl.pallas_call(
        functools.partial(_ffn_kernel, final_norm=final_norm),
        grid=(batch, nl),
        in_specs=[pl.BlockSpec((FFN_TL, d), tile),
                  pl.BlockSpec((FFN_TL, GDN_W), tile),
                  pl.BlockSpec((FFN_TL, MLSTM_W), tile),
                  pl.BlockSpec((FFN_TL, S5_W), tile),
                  resident(w_out.shape),
                  pl.BlockSpec((1, d), const),
                  resident(w_up.shape),
                  pl.BlockSpec(conv_w.shape, const),
                  pl.BlockSpec(conv_b.shape, const),
                  resident(w_down.shape),
                  pl.BlockSpec((1, d), const)],
        out_specs=pl.BlockSpec((FFN_TL, d), tile),
        out_shape=jax.ShapeDtypeStruct((t, d), F32),
        scratch_shapes=[pltpu.VMEM((FFN_TL, d), F32),
                        pltpu.VMEM((FFN_TL, d), BF16),
                        pltpu.VMEM((FFN_TL, D_FF), BF16),
                        pltpu.VMEM((FFN_TL + SUBLANES, FFN_PIECE), F32),
                        pltpu.VMEM((SUBLANES, 2 * D_FF), F32)],
        compiler_params=pltpu.CompilerParams(dimension_semantics=("arbitrary", "arbitrary"),
                                             vmem_limit_bytes=VMEM_LIMIT),
        name="ffn",
    )(xf, ya, yb, yc, w_out, norm2_w, w_up, conv_w, conv_b, w_down, final_w)


def _gate_row(vals, col):
    return jnp.zeros((1, GATE_W), F32).at[0, col:col + vals.shape[0]].set(vals.astype(F32))


def _layer(xf, batch, final_norm, norm1_w, w_in, gdn_conv_w, gdn_a_log, gdn_dt_bias, gdn_norm_w,
           mlstm_conv_w, mlstm_i_bias, mlstm_f_bias, mlstm_norm_w,
           s5_lam_re, s5_lam_im, s5_log_step, s5_b_re, s5_b_im, s5_c_re, s5_c_im,
           s5_d, s5_w_glu, w_out, norm2_w, w_up, ffn_conv_w, ffn_conv_b, w_down, final_norm_w):
    d = xf.shape[1]
    splits = (GDN_QK, GDN_QK, GDN_W, GDN_W, GDN_HEADS, GDN_HEADS,
              MLSTM_W, MLSTM_W, MLSTM_W, MLSTM_W, MLSTM_HEADS, MLSTM_HEADS, S5_W)
    offs = [0]
    for s in splits:
        offs.append(offs[-1] + s)
    seg = lambda i, j: w_in[:, offs[i]:offs[j]]
    gate_cols = jnp.concatenate(
        [seg(4, 6), seg(10, 12), jnp.zeros((d, GATE_W - 2 * GDN_HEADS - 2 * MLSTM_HEADS), F32)], axis=1)
    w_cat = jnp.concatenate([seg(0, 3), seg(3, 4), seg(6, 8), seg(8, 9), seg(9, 10), seg(12, 13),
                             gate_cols], axis=1).astype(BF16)
    widths = (3 * GDN_QK, GDN_W, 2 * MLSTM_W, MLSTM_W, MLSTM_W, S5_W, GATE_W)
    qkv, z, mqk, mv, mo, su, gates = _inproj(xf, norm1_w.reshape(1, d), w_cat, widths)

    ya = _gdn(qkv, z, gates, gdn_conv_w, _gate_row(gdn_a_log, GCOL_A), _gate_row(gdn_dt_bias, GCOL_A),
              gdn_norm_w.reshape(1, GDN_DV), batch)
    yb = _mlstm(mqk, mv, mo, gates, mlstm_conv_w, _gate_row(mlstm_i_bias, GCOL_I),
                _gate_row(mlstm_f_bias, GCOL_F), jnp.tile(mlstm_norm_w, MLSTM_HEADS).reshape(1, MLSTM_W), batch)
    bb, pw, pm, cc = _s5_tables(s5_lam_re, s5_lam_im, s5_log_step, s5_b_re, s5_b_im, s5_c_re, s5_c_im)
    yc = _s5(su, bb, pw, pm, cc, s5_d.reshape(1, S5_W), s5_w_glu.astype(BF16), batch)

    return _ffn(xf, ya, yb, yc, w_out.astype(BF16), norm2_w.reshape(1, d), w_up.astype(BF16),
                ffn_conv_w, ffn_conv_b.reshape(1, 2 * D_FF), w_down.astype(BF16),
                final_norm_w.reshape(1, d), batch, final_norm)


def kernel(x, norm1_w, w_in, gdn_conv_w, gdn_a_log, gdn_dt_bias, gdn_norm_w, mlstm_conv_w, mlstm_i_bias, mlstm_f_bias, mlstm_norm_w, s5_lam_re, s5_lam_im, s5_log_step, s5_b_re, s5_b_im, s5_c_re, s5_c_im, s5_d, s5_w_glu, w_out, norm2_w, w_up, ffn_conv_w, ffn_conv_b, w_down, final_norm_w):
    batch, seq, d = x.shape
    depth = w_in.shape[0]
    assert seq % FFN_TL == 0 and seq % MIX_TL == 0 and (batch * seq) % MM_TM == 0
    xf = x.reshape(batch * seq, d)
    per_layer = (norm1_w, w_in, gdn_conv_w, gdn_a_log, gdn_dt_bias, gdn_norm_w, mlstm_conv_w,
                 mlstm_i_bias, mlstm_f_bias, mlstm_norm_w, s5_lam_re, s5_lam_im, s5_log_step,
                 s5_b_re, s5_b_im, s5_c_re, s5_c_im, s5_d, s5_w_glu, w_out, norm2_w, w_up,
                 ffn_conv_w, ffn_conv_b, w_down)
    for l in range(depth):
        xf = _layer(xf, batch, l == depth - 1, *(p[l] for p in per_layer), final_norm_w)
    return xf.reshape(batch, seq, d)
```

```python
import functools
import math

import jax
import jax.numpy as jnp
from jax import lax
from jax.experimental import pallas as pl
from jax.experimental.pallas import tpu as pltpu

F32 = jnp.float32
BF16 = jnp.bfloat16

D_MODEL = 1024
CHUNK = 64
EPS = 1e-6
GDN_HEADS = 4
GDN_DK = 128
GDN_DV = 128
GDN_CONV = 4
GDN_QK = GDN_HEADS * GDN_DK
GDN_W = GDN_HEADS * GDN_DV
MLSTM_HEADS = 4
MLSTM_DH = 64
MLSTM_CONV = 4
MLSTM_W = MLSTM_HEADS * MLSTM_DH
S5_GROUPS = 16
S5_GROUP_CH = 16
S5_STATE = 64
S5_W = S5_GROUPS * S5_GROUP_CH
S5_NS = S5_GROUPS * S5_STATE
D_FF = 2816
FFN_CONV = 3

LANES = 128
SUBLANES = 8
GATE_W = LANES
GCOL_B, GCOL_A, GCOL_I, GCOL_F = 0, 4, 8, 12

MIX_TL = 256
FFN_TL = 512
FFN_PIECE = 256
VMEM_LIMIT = 56 * 1024 * 1024


def _dot(a, b):
    return jnp.dot(a.astype(BF16), b.astype(BF16), preferred_element_type=F32)


def _dot_nt(a, b):
    return lax.dot_general(a.astype(BF16), b.astype(BF16), (((1,), (1,)), ((), ())),
                           preferred_element_type=F32)


def _dot_tn(a, b):
    return lax.dot_general(a.astype(BF16), b.astype(BF16), (((0,), (0,)), ((), ())),
                           preferred_element_type=F32)


def _sigmoid(x):
    return 1.0 / (1.0 + jnp.exp(-x))


def _silu(x):
    return x * _sigmoid(x)


def _softplus(x):
    return jnp.maximum(x, 0.0) + jnp.log1p(jnp.exp(-jnp.abs(x)))


def _rms(x, w):
    return x * lax.rsqrt(jnp.mean(x * x, axis=-1, keepdims=True) + EPS) * w


def _cumsum_rows(g):
    pos = lax.broadcasted_iota(jnp.int32, g.shape, 0)
    k = 1
    while k < g.shape[0]:
        g = g + jnp.where(pos >= k, pltpu.roll(g, k, 0), 0.0)
        k *= 2
    return g


def _causal_conv(ext_ref, w_ref, cols, n_rows, k_w):
    acc = None
    for j in range(k_w):
        term = ext_ref[pl.ds(SUBLANES - (k_w - 1) + j, n_rows), cols] * w_ref[j:j + 1, cols]
        acc = term if acc is None else acc + term
    return acc


TRI_BASE = 16


def _tri_inverse(ms):
    n = ms[0].shape[0]
    row_i = lax.broadcasted_iota(jnp.int32, (n, n), 0)
    col_i = lax.broadcasted_iota(jnp.int32, (n, n), 1)

    def same_block(size):
        return (row_i // size) == (col_i // size)

    eye = (row_i == col_i).astype(F32)
    diag = same_block(TRI_BASE)
    mks = [jnp.where(diag, m, 0.0) for m in ms]
    xs = [eye - mk for mk in mks]
    k = 2
    while k < TRI_BASE:
        mks = [_dot(mk, mk) for mk in mks]
        xs = [x + _dot(x, mk) for x, mk in zip(xs, mks)]
        k *= 2
    size = TRI_BASE
    while size < n:
        off = same_block(2 * size) & jnp.logical_not(same_block(size))
        cxs = [_dot(jnp.where(off, m, 0.0), x) for m, x in zip(ms, xs)]
        xs = [x - _dot(x, cx) for x, cx in zip(xs, cxs)]
        size *= 2
    return xs


def _gdn_body(ext_ref, z, gates, cw_ref, alog_ref, dtb_ref, nw_ref, y_ref, s_ref):
    tl = z.shape[0]
    beta_all = _sigmoid(gates)
    g_all = -jnp.exp(alog_ref[...]) * _softplus(gates + dtb_ref[...])
    gc = _cumsum_rows(g_all)
    gc_t = gc.T

    row_i = lax.broadcasted_iota(jnp.int32, (tl, tl), 0)
    col_i = lax.broadcasted_iota(jnp.int32, (tl, tl), 1)
    lower = col_i <= row_i
    strict = col_i < row_i

    def conv_head(slab):
        cols = slice(slab * LANES, (slab + 1) * LANES)
        return _silu(_causal_conv(ext_ref, cw_ref, cols, tl, GDN_CONV))

    ms, rhss, a_qks, q_decs, k_decs, g_lasts = [], [], [], [], [], []
    for h in range(GDN_HEADS):
        q = conv_head(h)
        k = conv_head(GDN_HEADS + h)
        v = conv_head(2 * GDN_HEADS + h)
        q = q * lax.rsqrt(jnp.sum(q * q, axis=-1, keepdims=True) + EPS) * (GDN_DK ** -0.5)
        k = k * lax.rsqrt(jnp.sum(k * k, axis=-1, keepdims=True) + EPS)
        beta = beta_all[:, GCOL_B + h:GCOL_B + h + 1]
        g_col = gc[:, GCOL_A + h:GCOL_A + h + 1]
        g_row = gc_t[GCOL_A + h:GCOL_A + h + 1, :]
        g_last = g_col[tl - 1:tl, :]
        decay = jnp.exp(jnp.where(lower, g_col - g_row, -jnp.inf))
        kb = k * beta
        eg = jnp.exp(g_col)
        ms.append(jnp.where(strict, _dot_nt(kb, k) * decay, 0.0))
        rhss.append(jnp.concatenate([v * beta, kb * eg], axis=1).astype(BF16))
        a_qks.append((_dot_nt(q, k) * decay).astype(BF16))
        q_decs.append((q * eg).astype(BF16))
        k_decs.append((k * jnp.exp(g_last - g_col)).astype(BF16))
        g_lasts.append(g_last)
    ext_ref[0:SUBLANES, :] = ext_ref[tl:tl + SUBLANES, :]

    t_invs = _tri_inverse(ms)

    for h in range(GDN_HEADS):
        uw = _dot(t_invs[h], rhss[h])
        s = s_ref[h]
        v_new = uw[:, :GDN_DV] - _dot(uw[:, GDN_DV:], s)
        o = _dot(q_decs[h], s) + _dot(a_qks[h], v_new)
        s_ref[h] = s * jnp.exp(g_lasts[h]) + _dot_tn(k_decs[h], v_new)
        cols = slice(h * GDN_DV, (h + 1) * GDN_DV)
        y_ref[:, cols] = _rms(o, nw_ref[...]) * _silu(z[:, cols])


def _mlstm_body(ext_ref, v, o_pre, gates, cw_ref, ib_ref, fb_ref, nw_ref, y_ref, y_off,
                c_ref, n_ref, m_ref):
    tl = v.shape[0]
    dh = MLSTM_DH
    width = MLSTM_W

    qk_conv = _silu(_causal_conv(ext_ref, cw_ref, slice(None), tl, MLSTM_CONV))
    ext_ref[0:SUBLANES, :] = ext_ref[tl:tl + SUBLANES, :]
    q = qk_conv[:, :width] * (dh ** -0.5)
    k = qk_conv[:, width:]
    q_bf = q.astype(BF16)
    k_bf = k.astype(BF16)
    v_bf = v.astype(BF16)

    i_all = gates + ib_ref[...]
    logf_all = -_softplus(-(gates + fb_ref[...]))
    b_all = _cumsum_rows(logf_all)
    b_t = b_all.T
    i_t = i_all.T

    tril = (lax.broadcasted_iota(jnp.int32, (tl, tl), 1)
            <= lax.broadcasted_iota(jnp.int32, (tl, tl), 0))
    lane_head = lax.broadcasted_iota(jnp.int32, (1, width), 1) // dh
    c_st = c_ref[...]
    n_st = n_ref[...]
    q_c = _dot(q_bf, c_st)
    q_n = q * n_st

    hh = jnp.zeros((tl, width), F32)
    wk_scale = jnp.zeros((tl, width), F32)
    a_lanes = jnp.zeros((1, width), F32)
    for h in range(MLSTM_HEADS):
        in_head = lane_head == h
        b_col = b_all[:, GCOL_F + h:GCOL_F + h + 1]
        i_col = i_all[:, GCOL_I + h:GCOL_I + h + 1]
        b_row = b_t[GCOL_F + h:GCOL_F + h + 1, :]
        i_row = i_t[GCOL_I + h:GCOL_I + h + 1, :]
        b_last = b_col[tl - 1:tl, :]
        m_st = m_ref[h]
        src = i_row - b_row
        log_w = jnp.where(tril, b_col + src, -jnp.inf)
        m_intra = jnp.max(log_w, axis=-1, keepdims=True)
        log_inter = b_col + m_st
        m_t = jnp.maximum(log_inter, m_intra)
        w_inter = jnp.exp(log_inter - m_t)
        qk = _dot_nt(jnp.where(in_head, q_bf, jnp.zeros_like(q_bf)), k_bf)
        w_intra = jnp.exp(log_w - m_t) * qk
        num = w_inter * q_c + _dot(w_intra, v_bf)
        den = (w_inter * jnp.sum(jnp.where(in_head, q_n, 0.0), axis=-1, keepdims=True)
               + jnp.sum(w_intra, axis=-1, keepdims=True))
        out = num / jnp.maximum(jnp.abs(den), jnp.exp(-m_t))
        hh = jnp.where(in_head, out, hh)

        m_new = jnp.maximum(b_last + m_st, jnp.max(b_last + src, axis=-1, keepdims=True))
        a_lanes = jnp.where(in_head, jnp.exp(b_last + m_st - m_new), a_lanes)
        wk_scale = jnp.where(in_head, jnp.exp(b_last - b_col + i_col - m_new), wk_scale)
        m_ref[h] = m_new

    wk = wk_scale * k
    row_head = lax.broadcasted_iota(jnp.int32, (width, 1), 0) // dh
    c_ref[...] = a_lanes * c_st + jnp.where(row_head == lane_head, _dot_tn(wk, v_bf), 0.0)
    n_ref[...] = a_lanes * n_st + jnp.sum(wk, axis=0, keepdims=True)

    sq = hh * hh
    rs = jnp.zeros((tl, width), F32)
    for h in range(MLSTM_HEADS):
        in_head = lane_head == h
        ms = jnp.sum(jnp.where(in_head, sq, 0.0), axis=-1, keepdims=True) * (1.0 / dh)
        rs = jnp.where(in_head, lax.rsqrt(ms + EPS), rs)
    y_ref[:, y_off:y_off + width] = _sigmoid(o_pre) * (hh * rs * nw_ref[...])


S5_MINI = 4
S5_GROUP = S5_MINI * SUBLANES


def _cmul_add(ar, ai, xr, xi, br, bi):
    return ar * xr - ai * xi + br, ar * xi + ai * xr + bi


def _s5_body(u, bb_ref, pw_ref, pm_ref, cc_ref, d_ref, wg_ref, y_ref, y_off,
             st_ref, bu_ref, xb_ref):
    tl = u.shape[0]
    ns = S5_NS

    bu = _dot(u, bb_ref[...])
    n_lt = ns // LANES
    for c in range(2 * n_lt):
        bu_ref[c] = bu[:, c * LANES:(c + 1) * LANES]

    sub = lax.broadcasted_iota(jnp.int32, (SUBLANES, LANES), 0)
    for c in range(n_lt):
        re_c, im_c = c, n_lt + c
        lanes_r = slice(c * LANES, (c + 1) * LANES)
        lanes_i = slice(ns + c * LANES, ns + (c + 1) * LANES)
        bcast = lambda row: jnp.broadcast_to(row, (SUBLANES, LANES))
        a_r = [bcast(pw_ref[j:j + 1, lanes_r]) for j in range(S5_MINI)]
        a_i = [bcast(pw_ref[j:j + 1, lanes_i]) for j in range(S5_MINI)]
        gin_r = st_ref[0:1, lanes_r]
        gin_i = st_ref[0:1, lanes_i]
        for g in range(tl // S5_GROUP):
            base = g * S5_GROUP
            rows = [pl.ds(base + j, SUBLANES, stride=S5_MINI) for j in range(S5_MINI)]
            xr = [bu_ref[re_c, rows[0], :]]
            xi = [bu_ref[im_c, rows[0], :]]
            for j in range(1, S5_MINI):
                nr, ni = _cmul_add(a_r[0], a_i[0], xr[-1], xi[-1],
                                   bu_ref[re_c, rows[j], :], bu_ref[im_c, rows[j], :])
                xr.append(nr)
                xi.append(ni)
            er, ei = xr[-1], xi[-1]
            k, j = 1, S5_MINI - 1
            while k < SUBLANES:
                keep = sub >= k
                sr = jnp.where(keep, pltpu.roll(er, k, 0), 0.0)
                si = jnp.where(keep, pltpu.roll(ei, k, 0), 0.0)
                er, ei = _cmul_add(pw_ref[j:j + 1, lanes_r], pw_ref[j:j + 1, lanes_i], sr, si, er, ei)
                k *= 2
                j += 1
            er, ei = _cmul_add(pm_ref[:, lanes_r], pm_ref[:, lanes_i], gin_r, gin_i, er, ei)
            pr = jnp.where(sub == 0, gin_r, pltpu.roll(er, 1, 0))
            pi = jnp.where(sub == 0, gin_i, pltpu.roll(ei, 1, 0))
            gin_r = er[SUBLANES - 1:SUBLANES, :]
            gin_i = ei[SUBLANES - 1:SUBLANES, :]
            for j in range(S5_MINI):
                fr, fi = _cmul_add(a_r[j], a_i[j], pr, pi, xr[j], xi[j])
                bu_ref[re_c, rows[j], :] = fr
                bu_ref[im_c, rows[j], :] = fi
        st_ref[0:1, lanes_r] = gin_r
        st_ref[0:1, lanes_i] = gin_i

    for c in range(2 * n_lt):
        xb_ref[:, c * LANES:(c + 1) * LANES] = bu_ref[c].astype(BF16)

    y = jnp.dot(xb_ref[...], cc_ref[...], preferred_element_type=F32) + d_ref[...] * u
    g = 0.5 * y * (1.0 + lax.erf(y * (2.0 ** -0.5)))
    y_ref[:, y_off:y_off + S5_W] = g * _sigmoid(_dot(g, wg_ref[...]))


COL_QKV = 0
COL_Z = COL_QKV + 3 * GDN_QK
COL_MQK = COL_Z + GDN_W
COL_MV = COL_MQK + 2 * MLSTM_W
COL_MO = COL_MV + MLSTM_W
COL_SU = COL_MO + MLSTM_W
COL_GATES = COL_SU + S5_W
N_PROJ = COL_GATES + GATE_W


def _mix_kernel(x_ref, nw_ref, w_ref,
                gcw_ref, alog_ref, dtb_ref, gnw_ref,
                mcw_ref, ib_ref, fb_ref, mnw_ref,
                bb_ref, pw_ref, pm_ref, cc_ref, d_ref, wg_ref,
                y_ref,
                gext_ref, s_ref, mext_ref, c_ref, n_ref, m_ref, st_ref, bu_ref, xb_ref):
    tl = x_ref.shape[0]

    @pl.when(pl.program_id(1) == 0)
    def _():
        gext_ref[0:SUBLANES, :] = jnp.zeros((SUBLANES, gext_ref.shape[1]), F32)
        mext_ref[0:SUBLANES, :] = jnp.zeros((SUBLANES, mext_ref.shape[1]), F32)
        for ref in (s_ref, c_ref, n_ref, m_ref, st_ref):
            ref[...] = jnp.zeros(ref.shape, F32)

    h = _rms(x_ref[...], nw_ref[...]).astype(BF16)

    def proj(lo, width):
        return jnp.dot(h, w_ref[:, lo:lo + width], preferred_element_type=F32)

    gext_ref[SUBLANES:SUBLANES + tl, :] = proj(COL_QKV, 3 * GDN_QK)
    mext_ref[SUBLANES:SUBLANES + tl, :] = proj(COL_MQK, 2 * MLSTM_W)
    gates = proj(COL_GATES, GATE_W)

    _gdn_body(gext_ref, proj(COL_Z, GDN_W), gates, gcw_ref, alog_ref, dtb_ref, gnw_ref, y_ref, s_ref)
    _mlstm_body(mext_ref, proj(COL_MV, MLSTM_W), proj(COL_MO, MLSTM_W), gates, mcw_ref, ib_ref, fb_ref,
                mnw_ref, y_ref, GDN_W, c_ref, n_ref, m_ref)
    _s5_body(proj(COL_SU, S5_W), bb_ref, pw_ref, pm_ref, cc_ref, d_ref, wg_ref, y_ref, GDN_W + MLSTM_W,
             st_ref, bu_ref, xb_ref)


def _mixers(xf, batch, norm_w, w_cat, gdn_p, mlstm_p, s5_p):
    t, d = xf.shape
    nl = t // batch // MIX_TL
    tile = lambda b, l: (b * nl + l, 0)
    const = lambda b, l: (0, 0)
    params = (norm_w,) + tuple(gdn_p) + tuple(mlstm_p) + tuple(s5_p)
    return pl.pallas_call(
        _mix_kernel,
        grid=(batch, nl),
        in_specs=[pl.BlockSpec((MIX_TL, d), tile),
                  pl.BlockSpec(norm_w.shape, const),
                  pl.BlockSpec(w_cat.shape, const, pipeline_mode=pl.Buffered(1))]
                 + [pl.BlockSpec(p.shape, const) for p in params[1:]],
        out_specs=pl.BlockSpec((MIX_TL, d), tile),
        out_shape=jax.ShapeDtypeStruct((t, d), F32),
        scratch_shapes=[pltpu.VMEM((MIX_TL + SUBLANES, 3 * GDN_QK), F32),
                        pltpu.VMEM((GDN_HEADS, GDN_DK, GDN_DV), F32),
                        pltpu.VMEM((MIX_TL + SUBLANES, 2 * MLSTM_W), F32),
                        pltpu.VMEM((MLSTM_W, MLSTM_W), F32),
                        pltpu.VMEM((1, MLSTM_W), F32),
                        pltpu.VMEM((MLSTM_HEADS, 1, 1), F32),
                        pltpu.VMEM((SUBLANES, 2 * S5_NS), F32),
                        pltpu.VMEM((2 * S5_NS // LANES, MIX_TL, LANES), F32),
                        pltpu.VMEM((MIX_TL, 2 * S5_NS), BF16)],
        compiler_params=pltpu.CompilerParams(dimension_semantics=("arbitrary", "arbitrary"),
                                             vmem_limit_bytes=VMEM_LIMIT),
        name="mixers",
    )(xf, norm_w, w_cat, *params[1:])


def _s5_tables(lam_re, lam_im, log_step, b_re, b_im, c_re, c_im):
    step = jnp.exp(log_step)
    er = jnp.exp(lam_re * step)
    a_re = er * jnp.cos(lam_im * step)
    a_im = er * jnp.sin(lam_im * step)
    den = lam_re * lam_re + lam_im * lam_im
    coef_re = ((a_re - 1.0) * lam_re + a_im * lam_im) / den
    coef_im = (a_im * lam_re - (a_re - 1.0) * lam_im) / den
    bb_re = coef_re[..., None] * b_re - coef_im[..., None] * b_im
    bb_im = coef_re[..., None] * b_im + coef_im[..., None] * b_re
    eye = jnp.eye(S5_GROUPS, dtype=F32)
    blk_in = lambda m: jnp.einsum('gph,gk->ghkp', m, eye).reshape(S5_W, S5_NS)
    blk_out = lambda m: jnp.einsum('gjp,gk->gpkj', m, eye).reshape(S5_NS, S5_W)
    bb = jnp.concatenate([blk_in(bb_re), blk_in(bb_im)], axis=1).astype(BF16)
    cc = jnp.concatenate([blk_out(c_re), -blk_out(c_im)], axis=0).astype(BF16)

    ar = a_re.reshape(1, S5_NS)
    ai = a_im.reshape(1, S5_NS)
    tab_r, tab_i = ar, ai
    while tab_r.shape[0] < S5_GROUP:
        r, i = tab_r[-1:], tab_i[-1:]
        tab_r, tab_i = (jnp.concatenate([tab_r, tab_r * r - tab_i * i], axis=0),
                        jnp.concatenate([tab_i, tab_r * i + tab_i * r], axis=0))
    ptab = jnp.concatenate([tab_r, tab_i], axis=1)
    steps = list(range(S5_MINI)) + [S5_MINI * (2 << j) - 1 for j in range(int(math.log2(SUBLANES)) - 1)]
    pw = jnp.concatenate([ptab[jnp.array(steps)],
                          jnp.zeros((SUBLANES - len(steps), 2 * S5_NS), F32)], axis=0)
    pm = ptab[S5_MINI - 1::S5_MINI]
    return bb, pw, pm, cc


def _ffn_kernel(x_ref, y_ref, wo_ref, n2_ref, wup_ref, cw_ref, cb_ref,
                wdn_ref, fn_ref, out_ref, acc_ref, h_ref, act_ref, ext_ref, carry_ref, *, final_norm):
    tl = x_ref.shape[0]

    @pl.when(pl.program_id(1) == 0)
    def _():
        carry_ref[...] = jnp.zeros(carry_ref.shape, F32)

    x_mid = x_ref[...] + _dot(y_ref[...], wo_ref[...])
    acc_ref[...] = x_mid
    h_ref[...] = _rms(x_mid, n2_ref[...]).astype(BF16)

    def branch(cols):
        ext_ref[0:SUBLANES, :] = carry_ref[:, cols]
        ext_ref[SUBLANES:SUBLANES + tl, :] = jnp.dot(h_ref[...], wup_ref[:, cols],
                                                     preferred_element_type=F32)
        carry_ref[:, cols] = ext_ref[tl:tl + SUBLANES, :]
        acc = None
        for j in range(FFN_CONV):
            term = (ext_ref[pl.ds(SUBLANES - (FFN_CONV - 1) + j, tl), :]
                    * cw_ref[j:j + 1, cols])
            acc = term if acc is None else acc + term
        return acc + cb_ref[:, cols]

    for p in range(D_FF // FFN_PIECE):
        lo = p * FFN_PIECE
        gate = branch(slice(lo, lo + FFN_PIECE))
        up = branch(slice(D_FF + lo, D_FF + lo + FFN_PIECE))
        act_ref[:, lo:lo + FFN_PIECE] = (_silu(gate) * up).astype(BF16)

    out = acc_ref[...] + jnp.dot(act_ref[...], wdn_ref[...], preferred_element_type=F32)
    if final_norm:
        out = _rms(out, fn_ref[...])
    out_ref[...] = out


def _ffn(xf, y, w_out, norm2_w, w_up, conv_w, conv_b, w_down, final_w, batch, final_norm):
    t, d = xf.shape
    nl = t // batch // FFN_TL
    tile = lambda b, l: (b * nl + l, 0)
    const = lambda b, l: (0, 0)
    resident = functools.partial(pl.BlockSpec, index_map=const, pipeline_mode=pl.Buffered(1))
    return pl.pallas_call(
        functools.partial(_ffn_kernel, final_norm=final_norm),
        grid=(batch, nl),
        in_specs=[pl.BlockSpec((FFN_TL, d), tile),
                  pl.BlockSpec((FFN_TL, d), tile),
                  resident(w_out.shape),
                  pl.BlockSpec((1, d), const),
                  resident(w_up.shape),
                  pl.BlockSpec(conv_w.shape, const),
                  pl.BlockSpec(conv_b.shape, const),
                  resident(w_down.shape),
                  pl.BlockSpec((1, d), const)],
        out_specs=pl.BlockSpec((FFN_TL, d), tile),
        out_shape=jax.ShapeDtypeStruct((t, d), F32),
        scratch_shapes=[pltpu.VMEM((FFN_TL, d), F32),
                        pltpu.VMEM((FFN_TL, d), BF16),
                        pltpu.VMEM((FFN_TL, D_FF), BF16),
                        pltpu.VMEM((FFN_TL + SUBLANES, FFN_PIECE), F32),
                        pltpu.VMEM((SUBLANES, 2 * D_FF), F32)],
        compiler_params=pltpu.CompilerParams(dimension_semantics=("arbitrary", "arbitrary"),
                                             vmem_limit_bytes=VMEM_LIMIT),
        name="ffn",
    )(xf, y, w_out, norm2_w, w_up, conv_w, conv_b, w_down, final_w)


def _gate_row(vals, col):
    return jnp.zeros((1, GATE_W), F32).at[0, col:col + vals.shape[0]].set(vals.astype(F32))


def _layer(xf, batch, final_norm, norm1_w, w_in, gdn_conv_w, gdn_a_log, gdn_dt_bias, gdn_norm_w,
           mlstm_conv_w, mlstm_i_bias, mlstm_f_bias, mlstm_norm_w,
           s5_lam_re, s5_lam_im, s5_log_step, s5_b_re, s5_b_im, s5_c_re, s5_c_im,
           s5_d, s5_w_glu, w_out, norm2_w, w_up, ffn_conv_w, ffn_conv_b, w_down, final_norm_w):
    d = xf.shape[1]
    splits = (GDN_QK, GDN_QK, GDN_W, GDN_W, GDN_HEADS, GDN_HEADS,
              MLSTM_W, MLSTM_W, MLSTM_W, MLSTM_W, MLSTM_HEADS, MLSTM_HEADS, S5_W)
    offs = [0]
    for s in splits:
        offs.append(offs[-1] + s)
    seg = lambda i, j: w_in[:, offs[i]:offs[j]]
    gate_cols = jnp.concatenate(
        [seg(4, 6), seg(10, 12), jnp.zeros((d, GATE_W - 2 * GDN_HEADS - 2 * MLSTM_HEADS), F32)], axis=1)
    w_cat = jnp.concatenate([seg(0, 3), seg(3, 4), seg(6, 8), seg(8, 9), seg(9, 10), seg(12, 13),
                             gate_cols], axis=1).astype(BF16)
    assert w_cat.shape[1] == N_PROJ
    bb, pw, pm, cc = _s5_tables(s5_lam_re, s5_lam_im, s5_log_step, s5_b_re, s5_b_im, s5_c_re, s5_c_im)
    y = _mixers(
        xf, batch, norm1_w.reshape(1, d), w_cat,
        (gdn_conv_w, _gate_row(gdn_a_log, GCOL_A), _gate_row(gdn_dt_bias, GCOL_A),
         gdn_norm_w.reshape(1, GDN_DV)),
        (mlstm_conv_w, _gate_row(mlstm_i_bias, GCOL_I), _gate_row(mlstm_f_bias, GCOL_F),
         jnp.tile(mlstm_norm_w, MLSTM_HEADS).reshape(1, MLSTM_W)),
        (bb, pw, pm, cc, s5_d.reshape(1, S5_W), s5_w_glu.astype(BF16)))
    return _ffn(xf, y, w_out.astype(BF16), norm2_w.reshape(1, d), w_up.astype(BF16),
                ffn_conv_w, ffn_conv_b.reshape(1, 2 * D_FF), w_down.astype(BF16),
                final_norm_w.reshape(1, d), batch, final_norm)


def kernel(x, norm1_w, w_in, gdn_conv_w, gdn_a_log, gdn_dt_bias, gdn_norm_w, mlstm_conv_w, mlstm_i_bias, mlstm_f_bias, mlstm_norm_w, s5_lam_re, s5_lam_im, s5_log_step, s5_b_re, s5_b_im, s5_c_re, s5_c_im, s5_d, s5_w_glu, w_out, norm2_w, w_up, ffn_conv_w, ffn_conv_b, w_down, final_norm_w):
    batch, seq, d = x.shape
    depth = w_in.shape[0]
    assert seq % FFN_TL == 0 and seq % MIX_TL == 0
    xf = x.reshape(batch * seq, d)
    per_layer = (norm1_w, w_in, gdn_conv_w, gdn_a_log, gdn_dt_bias, gdn_norm_w, mlstm_conv_w,
                 mlstm_i_bias, mlstm_f_bias, mlstm_norm_w, s5_lam_re, s5_lam_im, s5_log_step,
                 s5_b_re, s5_b_im, s5_c_re, s5_c_im, s5_d, s5_w_glu, w_out, norm2_w, w_up,
                 ffn_conv_w, ffn_conv_b, w_down)
    for l in range(depth):
        xf = _layer(xf, batch, l == depth - 1, *(p[l] for p in per_layer), final_norm_w)
    return xf.reshape(batch, seq, d)
```

```python
import functools
import math

import jax
import jax.numpy as jnp
from jax import lax
from jax.experimental import pallas as pl
from jax.experimental.pallas import tpu as pltpu

F32 = jnp.float32
BF16 = jnp.bfloat16

D_MODEL = 1024
CHUNK = 64
EPS = 1e-6
GDN_HEADS = 4
GDN_DK = 128
GDN_DV = 128
GDN_CONV = 4
GDN_QK = GDN_HEADS * GDN_DK
GDN_W = GDN_HEADS * GDN_DV
MLSTM_HEADS = 4
MLSTM_DH = 64
MLSTM_CONV = 4
MLSTM_W = MLSTM_HEADS * MLSTM_DH
S5_GROUPS = 16
S5_GROUP_CH = 16
S5_STATE = 64
S5_W = S5_GROUPS * S5_GROUP_CH
S5_NS = S5_GROUPS * S5_STATE
D_FF = 2816
FFN_CONV = 3

LANES = 128
SUBLANES = 8
GATE_W = LANES
GCOL_B, GCOL_A, GCOL_I, GCOL_F = 0, 4, 8, 12

MIX_CHUNK = 256
MIX_TL = 512
FFN_TL = 512
FFN_PIECE = 256
VMEM_LIMIT = 56 * 1024 * 1024


def _dot(a, b):
    return jnp.dot(a.astype(BF16), b.astype(BF16), preferred_element_type=F32)


def _dot_nt(a, b):
    return lax.dot_general(a.astype(BF16), b.astype(BF16), (((1,), (1,)), ((), ())),
                           preferred_element_type=F32)


def _dot_tn(a, b):
    return lax.dot_general(a.astype(BF16), b.astype(BF16), (((0,), (0,)), ((), ())),
                           preferred_element_type=F32)


def _sigmoid(x):
    return 1.0 / (1.0 + jnp.exp(-x))


def _silu(x):
    return x * _sigmoid(x)


def _softplus(x):
    return jnp.maximum(x, 0.0) + jnp.log1p(jnp.exp(-jnp.abs(x)))


def _rms(x, w):
    return x * lax.rsqrt(jnp.mean(x * x, axis=-1, keepdims=True) + EPS) * w


def _cumsum_rows(g):
    pos = lax.broadcasted_iota(jnp.int32, g.shape, 0)
    k = 1
    while k < g.shape[0]:
        g = g + jnp.where(pos >= k, pltpu.roll(g, k, 0), 0.0)
        k *= 2
    return g


def _causal_conv(ext_ref, w_ref, cols, base, n_rows, k_w):
    acc = None
    for j in range(k_w):
        term = (ext_ref[pl.ds(base + SUBLANES - (k_w - 1) + j, n_rows), cols]
                * w_ref[j:j + 1, cols])
        acc = term if acc is None else acc + term
    return acc


TRI_BASE = 16


def _tri_inverse(ms):
    n = ms[0].shape[0]
    row_i = lax.broadcasted_iota(jnp.int32, (n, n), 0)
    col_i = lax.broadcasted_iota(jnp.int32, (n, n), 1)

    def same_block(size):
        return (row_i // size) == (col_i // size)

    def as_mask(cond):
        return jnp.where(cond, 1.0, 0.0).astype(BF16)

    def mm(a, b):
        return jnp.dot(a, b, preferred_element_type=F32).astype(BF16)

    m_bfs = [m.astype(BF16) for m in ms]
    diag = as_mask(same_block(TRI_BASE))
    eye = as_mask(row_i == col_i)
    mks = [m * diag for m in m_bfs]
    xs = [eye - mk for mk in mks]
    k = 2
    while k < TRI_BASE:
        mks = [mm(mk, mk) for mk in mks]
        xs = [x + mm(x, mk) for x, mk in zip(xs, mks)]
        k *= 2
    size = TRI_BASE
    while size < n:
        off = as_mask(same_block(2 * size) & jnp.logical_not(same_block(size)))
        cxs = [mm(m * off, x) for m, x in zip(m_bfs, xs)]
        xs = [x - mm(x, cx) for x, cx in zip(xs, cxs)]
        size *= 2
    return xs


def _gdn_body(ext_ref, base, z, gates, cw_ref, alog_ref, dtb_ref, nw_ref, y_ref, s_ref):
    tl = z.shape[0]
    beta_all = _sigmoid(gates)
    g_all = -jnp.exp(alog_ref[...]) * _softplus(gates + dtb_ref[...])
    gc = _cumsum_rows(g_all)
    gc_t = gc.T

    row_i = lax.broadcasted_iota(jnp.int32, (tl, tl), 0)
    col_i = lax.broadcasted_iota(jnp.int32, (tl, tl), 1)
    lower = col_i <= row_i
    strict = col_i < row_i

    def conv_head(slab):
        cols = slice(slab * LANES, (slab + 1) * LANES)
        return _silu(_causal_conv(ext_ref, cw_ref, cols, base, tl, GDN_CONV))

    ms, rhss, a_qks, q_decs, k_decs, g_lasts = [], [], [], [], [], []
    for h in range(GDN_HEADS):
        q = conv_head(h)
        k = conv_head(GDN_HEADS + h)
        v = conv_head(2 * GDN_HEADS + h)
        q = q * lax.rsqrt(jnp.sum(q * q, axis=-1, keepdims=True) + EPS) * (GDN_DK ** -0.5)
        k = k * lax.rsqrt(jnp.sum(k * k, axis=-1, keepdims=True) + EPS)
        beta = beta_all[:, GCOL_B + h:GCOL_B + h + 1]
        g_col = gc[:, GCOL_A + h:GCOL_A + h + 1]
        g_row = gc_t[GCOL_A + h:GCOL_A + h + 1, :]
        g_last = g_col[tl - 1:tl, :]
        decay = jnp.exp(jnp.where(lower, g_col - g_row, -jnp.inf))
        kb = k * beta
        eg = jnp.exp(g_col)
        ms.append(jnp.where(strict, _dot_nt(kb, k) * decay, 0.0))
        rhss.append(jnp.concatenate([v * beta, kb * eg], axis=1).astype(BF16))
        a_qks.append((_dot_nt(q, k) * decay).astype(BF16))
        q_decs.append((q * eg).astype(BF16))
        k_decs.append((k * jnp.exp(g_last - g_col)).astype(BF16))
        g_lasts.append(g_last)

    t_invs = _tri_inverse(ms)

    for h in range(GDN_HEADS):
        uw = _dot(t_invs[h], rhss[h])
        s = s_ref[h]
        v_new = uw[:, :GDN_DV] - _dot(uw[:, GDN_DV:], s)
        o = _dot(q_decs[h], s) + _dot(a_qks[h], v_new)
        s_ref[h] = s * jnp.exp(g_lasts[h]) + _dot_tn(k_decs[h], v_new)
        cols = slice(h * GDN_DV, (h + 1) * GDN_DV)
        y_ref[base:base + tl, cols] = (_rms(o, nw_ref[...]) * _silu(z[:, cols])).astype(y_ref.dtype)


def _mlstm_body(ext_ref, base, v, o_pre, gates, cw_ref, ib_ref, fb_ref, nw_ref, y_ref, y_off,
                c_ref, n_ref, m_ref):
    tl = v.shape[0]
    dh = MLSTM_DH
    width = MLSTM_W

    qk_conv = _silu(_causal_conv(ext_ref, cw_ref, slice(None), base, tl, MLSTM_CONV))
    q = qk_conv[:, :width] * (dh ** -0.5)
    k = qk_conv[:, width:]
    q_bf = q.astype(BF16)
    k_bf = k.astype(BF16)
    v_bf = v.astype(BF16)

    i_all = gates + ib_ref[...]
    logf_all = -_softplus(-(gates + fb_ref[...]))
    b_all = _cumsum_rows(logf_all)
    b_t = b_all.T
    i_t = i_all.T

    tril = (lax.broadcasted_iota(jnp.int32, (tl, tl), 1)
            <= lax.broadcasted_iota(jnp.int32, (tl, tl), 0))
    lane_head = lax.broadcasted_iota(jnp.int32, (1, width), 1) // dh
    c_st = c_ref[...]
    n_st = n_ref[...]
    q_c = _dot(q_bf, c_st)
    q_n = q * n_st

    hh = jnp.zeros((tl, width), F32)
    wk_scale = jnp.zeros((tl, width), F32)
    a_lanes = jnp.zeros((1, width), F32)
    for h in range(MLSTM_HEADS):
        in_head = lane_head == h
        b_col = b_all[:, GCOL_F + h:GCOL_F + h + 1]
        i_col = i_all[:, GCOL_I + h:GCOL_I + h + 1]
        b_row = b_t[GCOL_F + h:GCOL_F + h + 1, :]
        i_row = i_t[GCOL_I + h:GCOL_I + h + 1, :]
        b_last = b_col[tl - 1:tl, :]
        m_st = m_ref[h]
        src = i_row - b_row
        log_w = jnp.where(tril, b_col + src, -jnp.inf)
        m_intra = jnp.max(log_w, axis=-1, keepdims=True)
        log_inter = b_col + m_st
        m_t = jnp.maximum(log_inter, m_intra)
        w_inter = jnp.exp(log_inter - m_t)
        qk = _dot_nt(jnp.where(in_head, q_bf, jnp.zeros_like(q_bf)), k_bf)
        w_intra = jnp.exp(log_w - m_t) * qk
        num = w_inter * q_c + _dot(w_intra, v_bf)
        den = (w_inter * jnp.sum(jnp.where(in_head, q_n, 0.0), axis=-1, keepdims=True)
               + jnp.sum(w_intra, axis=-1, keepdims=True))
        out = num / jnp.maximum(jnp.abs(den), jnp.exp(-m_t))
        hh = jnp.where(in_head, out, hh)

        m_new = jnp.maximum(b_last + m_st, jnp.max(b_last + src, axis=-1, keepdims=True))
        a_lanes = jnp.where(in_head, jnp.exp(b_last + m_st - m_new), a_lanes)
        wk_scale = jnp.where(in_head, jnp.exp(b_last - b_col + i_col - m_new), wk_scale)
        m_ref[h] = m_new

    wk = wk_scale * k
    row_head = lax.broadcasted_iota(jnp.int32, (width, 1), 0) // dh
    c_ref[...] = a_lanes * c_st + jnp.where(row_head == lane_head, _dot_tn(wk, v_bf), 0.0)
    n_ref[...] = a_lanes * n_st + jnp.sum(wk, axis=0, keepdims=True)

    sq = hh * hh
    rs = jnp.zeros((tl, width), F32)
    for h in range(MLSTM_HEADS):
        in_head = lane_head == h
        ms = jnp.sum(jnp.where(in_head, sq, 0.0), axis=-1, keepdims=True) * (1.0 / dh)
        rs = jnp.where(in_head, lax.rsqrt(ms + EPS), rs)
    y_ref[base:base + tl, y_off:y_off + width] = (_sigmoid(o_pre) * (hh * rs * nw_ref[...])).astype(y_ref.dtype)


S5_MINI = 4
S5_GROUP = S5_MINI * SUBLANES


def _cmul_add(ar, ai, xr, xi, br, bi):
    return ar * xr - ai * xi + br, ar * xi + ai * xr + bi


def _s5_body(u, bb_ref, pw_ref, pm_ref, cc_ref, d_ref, wg_ref, y_ref, y_off,
             st_ref, bu_ref, xb_ref):
    tl = u.shape[0]
    ns = S5_NS

    bu = _dot(u, bb_ref[...])
    n_lt = ns // LANES
    for c in range(2 * n_lt):
        bu_ref[c] = bu[:, c * LANES:(c + 1) * LANES]

    sub = lax.broadcasted_iota(jnp.int32, (SUBLANES, LANES), 0)
    for c in range(n_lt):
        re_c, im_c = c, n_lt + c
        lanes_r = slice(c * LANES, (c + 1) * LANES)
        lanes_i = slice(ns + c * LANES, ns + (c + 1) * LANES)
        bcast = lambda row: jnp.broadcast_to(row, (SUBLANES, LANES))
        a_r = [bcast(pw_ref[j:j + 1, lanes_r]) for j in range(S5_MINI)]
        a_i = [bcast(pw_ref[j:j + 1, lanes_i]) for j in range(S5_MINI)]
        gin_r = st_ref[0:1, lanes_r]
        gin_i = st_ref[0:1, lanes_i]
        for g in range(tl // S5_GROUP):
            base = g * S5_GROUP
            rows = [pl.ds(base + j, SUBLANES, stride=S5_MINI) for j in range(S5_MINI)]
            xr = [bu_ref[re_c, rows[0], :]]
            xi = [bu_ref[im_c, rows[0], :]]
            for j in range(1, S5_MINI):
                nr, ni = _cmul_add(a_r[0], a_i[0], xr[-1], xi[-1],
                                   bu_ref[re_c, rows[j], :], bu_ref[im_c, rows[j], :])
                xr.append(nr)
                xi.append(ni)
            er, ei = xr[-1], xi[-1]
            k, j = 1, S5_MINI - 1
            while k < SUBLANES:
                keep = sub >= k
                sr = jnp.where(keep, pltpu.roll(er, k, 0), 0.0)
                si = jnp.where(keep, pltpu.roll(ei, k, 0), 0.0)
                er, ei = _cmul_add(pw_ref[j:j + 1, lanes_r], pw_ref[j:j + 1, lanes_i], sr, si, er, ei)
                k *= 2
                j += 1
            er, ei = _cmul_add(pm_ref[:, lanes_r], pm_ref[:, lanes_i], gin_r, gin_i, er, ei)
            pr = jnp.where(sub == 0, gin_r, pltpu.roll(er, 1, 0))
            pi = jnp.where(sub == 0, gin_i, pltpu.roll(ei, 1, 0))
            gin_r = er[SUBLANES - 1:SUBLANES, :]
            gin_i = ei[SUBLANES - 1:SUBLANES, :]
            for j in range(S5_MINI):
                fr, fi = _cmul_add(a_r[j], a_i[j], pr, pi, xr[j], xi[j])
                bu_ref[re_c, rows[j], :] = fr
                bu_ref[im_c, rows[j], :] = fi
        st_ref[0:1, lanes_r] = gin_r
        st_ref[0:1, lanes_i] = gin_i

    for c in range(2 * n_lt):
        xb_ref[:, c * LANES:(c + 1) * LANES] = bu_ref[c].astype(BF16)

    y = jnp.dot(xb_ref[...], cc_ref[...], preferred_element_type=F32) + d_ref[...] * u
    g = 0.5 * y * (1.0 + lax.erf(y * (2.0 ** -0.5)))
    y_ref[:, y_off:y_off + S5_W] = (g * _sigmoid(_dot(g, wg_ref[...]))).astype(y_ref.dtype)


COL_QKV = 0
COL_Z = COL_QKV + 3 * GDN_QK
COL_MQK = COL_Z + GDN_W
COL_MV = COL_MQK + 2 * MLSTM_W
COL_MO = COL_MV + MLSTM_W
COL_SU = COL_MO + MLSTM_W
COL_GATES = COL_SU + S5_W
N_PROJ = COL_GATES + GATE_W


def _mix_kernel(x_ref, nw_ref, w_ref,
                gcw_ref, alog_ref, dtb_ref, gnw_ref,
                mcw_ref, ib_ref, fb_ref, mnw_ref,
                bb_ref, pw_ref, pm_ref, cc_ref, d_ref, wg_ref,
                y_ref,
                gext_ref, s_ref, mext_ref, c_ref, n_ref, m_ref, st_ref, bu_ref, xb_ref):
    tl = x_ref.shape[0]

    @pl.when(pl.program_id(1) == 0)
    def _():
        gext_ref[0:SUBLANES, :] = jnp.zeros((SUBLANES, gext_ref.shape[1]), F32)
        mext_ref[0:SUBLANES, :] = jnp.zeros((SUBLANES, mext_ref.shape[1]), F32)
        for ref in (s_ref, c_ref, n_ref, m_ref, st_ref):
            ref[...] = jnp.zeros(ref.shape, F32)

    h = _rms(x_ref[...], nw_ref[...]).astype(BF16)

    def proj(lo, width):
        return jnp.dot(h, w_ref[:, lo:lo + width], preferred_element_type=F32)

    gext_ref[SUBLANES:SUBLANES + tl, :] = proj(COL_QKV, 3 * GDN_QK)
    mext_ref[SUBLANES:SUBLANES + tl, :] = proj(COL_MQK, 2 * MLSTM_W)
    gates = proj(COL_GATES, GATE_W)
    z = proj(COL_Z, GDN_W)
    m_v = proj(COL_MV, MLSTM_W)
    m_o = proj(COL_MO, MLSTM_W)

    for base in range(0, tl, MIX_CHUNK):
        r = slice(base, base + MIX_CHUNK)
        _gdn_body(gext_ref, base, z[r], gates[r], gcw_ref, alog_ref, dtb_ref, gnw_ref, y_ref, s_ref)
        _mlstm_body(mext_ref, base, m_v[r], m_o[r], gates[r], mcw_ref, ib_ref, fb_ref,
                    mnw_ref, y_ref, GDN_W, c_ref, n_ref, m_ref)
    gext_ref[0:SUBLANES, :] = gext_ref[tl:tl + SUBLANES, :]
    mext_ref[0:SUBLANES, :] = mext_ref[tl:tl + SUBLANES, :]
    _s5_body(proj(COL_SU, S5_W), bb_ref, pw_ref, pm_ref, cc_ref, d_ref, wg_ref, y_ref, GDN_W + MLSTM_W,
             st_ref, bu_ref, xb_ref)


def _mixers(xf, batch, norm_w, w_cat, gdn_p, mlstm_p, s5_p):
    t, d = xf.shape
    nl = t // batch // MIX_TL
    tile = lambda b, l: (b * nl + l, 0)
    const = lambda b, l: (0, 0)
    params = (norm_w,) + tuple(gdn_p) + tuple(mlstm_p) + tuple(s5_p)
    return pl.pallas_call(
        _mix_kernel,
        grid=(batch, nl),
        in_specs=[pl.BlockSpec((MIX_TL, d), tile),
                  pl.BlockSpec(norm_w.shape, const),
                  pl.BlockSpec(w_cat.shape, const, pipeline_mode=pl.Buffered(1))]
                 + [pl.BlockSpec(p.shape, const) for p in params[1:]],
        out_specs=pl.BlockSpec((MIX_TL, d), tile),
        out_shape=jax.ShapeDtypeStruct((t, d), BF16),
        scratch_shapes=[pltpu.VMEM((MIX_TL + SUBLANES, 3 * GDN_QK), F32),
                        pltpu.VMEM((GDN_HEADS, GDN_DK, GDN_DV), F32),
                        pltpu.VMEM((MIX_TL + SUBLANES, 2 * MLSTM_W), F32),
                        pltpu.VMEM((MLSTM_W, MLSTM_W), F32),
                        pltpu.VMEM((1, MLSTM_W), F32),
                        pltpu.VMEM((MLSTM_HEADS, 1, 1), F32),
                        pltpu.VMEM((SUBLANES, 2 * S5_NS), F32),
                        pltpu.VMEM((2 * S5_NS // LANES, MIX_TL, LANES), F32),
                        pltpu.VMEM((MIX_TL, 2 * S5_NS), BF16)],
        compiler_params=pltpu.CompilerParams(dimension_semantics=("arbitrary", "arbitrary"),
                                             vmem_limit_bytes=VMEM_LIMIT),
        name="mixers",
    )(xf, norm_w, w_cat, *params[1:])


def _s5_tables(lam_re, lam_im, log_step, b_re, b_im, c_re, c_im):
    step = jnp.exp(log_step)
    er = jnp.exp(lam_re * step)
    a_re = er * jnp.cos(lam_im * step)
    a_im = er * jnp.sin(lam_im * step)
    den = lam_re * lam_re + lam_im * lam_im
    coef_re = ((a_re - 1.0) * lam_re + a_im * lam_im) / den
    coef_im = (a_im * lam_re - (a_re - 1.0) * lam_im) / den
    bb_re = coef_re[..., None] * b_re - coef_im[..., None] * b_im
    bb_im = coef_re[..., None] * b_im + coef_im[..., None] * b_re
    eye = jnp.eye(S5_GROUPS, dtype=F32)
    blk_in = lambda m: jnp.einsum('gph,gk->ghkp', m, eye).reshape(S5_W, S5_NS)
    blk_out = lambda m: jnp.einsum('gjp,gk->gpkj', m, eye).reshape(S5_NS, S5_W)
    bb = jnp.concatenate([blk_in(bb_re), blk_in(bb_im)], axis=1).astype(BF16)
    cc = jnp.concatenate([blk_out(c_re), -blk_out(c_im)], axis=0).astype(BF16)

    ar = a_re.reshape(1, S5_NS)
    ai = a_im.reshape(1, S5_NS)
    tab_r, tab_i = ar, ai
    while tab_r.shape[0] < S5_GROUP:
        r, i = tab_r[-1:], tab_i[-1:]
        tab_r, tab_i = (jnp.concatenate([tab_r, tab_r * r - tab_i * i], axis=0),
                        jnp.concatenate([tab_i, tab_r * i + tab_i * r], axis=0))
    ptab = jnp.concatenate([tab_r, tab_i], axis=1)
    steps = list(range(S5_MINI)) + [S5_MINI * (2 << j) - 1 for j in range(int(math.log2(SUBLANES)) - 1)]
    pw = jnp.concatenate([ptab[jnp.array(steps)],
                          jnp.zeros((SUBLANES - len(steps), 2 * S5_NS), F32)], axis=0)
    pm = ptab[S5_MINI - 1::S5_MINI]
    return bb, pw, pm, cc


def _ffn_kernel(x_ref, y_ref, wo_ref, n2_ref, wup_ref, cw_ref, cb_ref,
                wdn_ref, fn_ref, out_ref, h_ref, act_ref, ext_ref, carry_ref, *, final_norm):
    tl = x_ref.shape[0]

    @pl.when(pl.program_id(1) == 0)
    def _():
        carry_ref[...] = jnp.zeros(carry_ref.shape, F32)

    x_mid = x_ref[...] + _dot(y_ref[...], wo_ref[...])
    out_ref[...] = x_mid
    h_ref[...] = _rms(x_mid, n2_ref[...]).astype(BF16)

    def branch(cols):
        ext_ref[0:SUBLANES, :] = carry_ref[:, cols]
        ext_ref[SUBLANES:SUBLANES + tl, :] = jnp.dot(h_ref[...], wup_ref[:, cols],
                                                     preferred_element_type=F32)
        carry_ref[:, cols] = ext_ref[tl:tl + SUBLANES, :]
        acc = None
        for j in range(FFN_CONV):
            term = (ext_ref[pl.ds(SUBLANES - (FFN_CONV - 1) + j, tl), :]
                    * cw_ref[j:j + 1, cols])
            acc = term if acc is None else acc + term
        return acc + cb_ref[:, cols]

    for p in range(D_FF // FFN_PIECE):
        lo = p * FFN_PIECE
        gate = branch(slice(lo, lo + FFN_PIECE))
        up = branch(slice(D_FF + lo, D_FF + lo + FFN_PIECE))
        act_ref[:, lo:lo + FFN_PIECE] = (_silu(gate) * up).astype(BF16)

    out = out_ref[...] + jnp.dot(act_ref[...], wdn_ref[...], preferred_element_type=F32)
    if final_norm:
        out = _rms(out, fn_ref[...])
    out_ref[...] = out


def _ffn(xf, y, w_out, norm2_w, w_up, conv_w, conv_b, w_down, final_w, batch, final_norm):
    t, d = xf.shape
    nl = t // batch // FFN_TL
    tile = lambda b, l: (b * nl + l, 0)
    const = lambda b, l: (0, 0)
    resident = functools.partial(pl.BlockSpec, index_map=const, pipeline_mode=pl.Buffered(1))
    return pl.pallas_call(
        functools.partial(_ffn_kernel, final_norm=final_norm),
        grid=(batch, nl),
        in_specs=[pl.BlockSpec((FFN_TL, d), tile),
                  pl.BlockSpec((FFN_TL, d), tile),
                  resident(w_out.shape),
                  pl.BlockSpec((1, d), const),
                  resident(w_up.shape),
                  pl.BlockSpec(conv_w.shape, const),
                  pl.BlockSpec(conv_b.shape, const),
                  resident(w_down.shape),
                  pl.BlockSpec((1, d), const)],
        out_specs=pl.BlockSpec((FFN_TL, d), tile),
        out_shape=jax.ShapeDtypeStruct((t, d), F32),
        scratch_shapes=[pltpu.VMEM((FFN_TL, d), BF16),
                        pltpu.VMEM((FFN_TL, D_FF), BF16),
                        pltpu.VMEM((FFN_TL + SUBLANES, FFN_PIECE), F32),
                        pltpu.VMEM((SUBLANES, 2 * D_FF), F32)],
        compiler_params=pltpu.CompilerParams(dimension_semantics=("arbitrary", "arbitrary"),
                                             vmem_limit_bytes=VMEM_LIMIT),
        name="ffn",
    )(xf, y, w_out, norm2_w, w_up, conv_w, conv_b, w_down, final_w)


def _gate_row(vals, col):
    return jnp.zeros((1, GATE_W), F32).at[0, col:col + vals.shape[0]].set(vals.astype(F32))


def _layer(xf, batch, final_norm, norm1_w, w_in, gdn_conv_w, gdn_a_log, gdn_dt_bias, gdn_norm_w,
           mlstm_conv_w, mlstm_i_bias, mlstm_f_bias, mlstm_norm_w,
           s5_lam_re, s5_lam_im, s5_log_step, s5_b_re, s5_b_im, s5_c_re, s5_c_im,
           s5_d, s5_w_glu, w_out, norm2_w, w_up, ffn_conv_w, ffn_conv_b, w_down, final_norm_w):
    d = xf.shape[1]
    splits = (GDN_QK, GDN_QK, GDN_W, GDN_W, GDN_HEADS, GDN_HEADS,
              MLSTM_W, MLSTM_W, MLSTM_W, MLSTM_W, MLSTM_HEADS, MLSTM_HEADS, S5_W)
    offs = [0]
    for s in splits:
        offs.append(offs[-1] + s)
    seg = lambda i, j: w_in[:, offs[i]:offs[j]]
    gate_cols = jnp.concatenate(
        [seg(4, 6), seg(10, 12), jnp.zeros((d, GATE_W - 2 * GDN_HEADS - 2 * MLSTM_HEADS), F32)], axis=1)
    w_cat = jnp.concatenate([seg(0, 3), seg(3, 4), seg(6, 8), seg(8, 9), seg(9, 10), seg(12, 13),
                             gate_cols], axis=1).astype(BF16)
    assert w_cat.shape[1] == N_PROJ
    bb, pw, pm, cc = _s5_tables(s5_lam_re, s5_lam_im, s5_log_step, s5_b_re, s5_b_im, s5_c_re, s5_c_im)
    y = _mixers(
        xf, batch, norm1_w.reshape(1, d), w_cat,
        (gdn_conv_w, _gate_row(gdn_a_log, GCOL_A), _gate_row(gdn_dt_bias, GCOL_A),
         gdn_norm_w.reshape(1, GDN_DV)),
        (mlstm_conv_w, _gate_row(mlstm_i_bias, GCOL_I), _gate_row(mlstm_f_bias, GCOL_F),
         jnp.tile(mlstm_norm_w, MLSTM_HEADS).reshape(1, MLSTM_W)),
        (bb, pw, pm, cc, s5_d.reshape(1, S5_W), s5_w_glu.astype(BF16)))
    return _ffn(xf, y, w_out.astype(BF16), norm2_w.reshape(1, d), w_up.astype(BF16),
                ffn_conv_w, ffn_conv_b.reshape(1, 2 * D_FF), w_down.astype(BF16),
                final_norm_w.reshape(1, d), batch, final_norm)


def kernel(x, norm1_w, w_in, gdn_conv_w, gdn_a_log, gdn_dt_bias, gdn_norm_w, mlstm_conv_w, mlstm_i_bias, mlstm_f_bias, mlstm_norm_w, s5_lam_re, s5_lam_im, s5_log_step, s5_b_re, s5_b_im, s5_c_re, s5_c_im, s5_d, s5_w_glu, w_out, norm2_w, w_up, ffn_conv_w, ffn_conv_b, w_down, final_norm_w):
    batch, seq, d = x.shape
    depth = w_in.shape[0]
    assert seq % FFN_TL == 0 and seq % MIX_TL == 0
    xf = x.reshape(batch * seq, d)
    per_layer = (norm1_w, w_in, gdn_conv_w, gdn_a_log, gdn_dt_bias, gdn_norm_w, mlstm_conv_w,
                 mlstm_i_bias, mlstm_f_bias, mlstm_norm_w, s5_lam_re, s5_lam_im, s5_log_step,
                 s5_b_re, s5_b_im, s5_c_re, s5_c_im, s5_d, s5_w_glu, w_out, norm2_w, w_up,
                 ffn_conv_w, ffn_conv_b, w_down)
    for l in range(depth):
        xf = _layer(xf, batch, l == depth - 1, *(p[l] for p in per_layer), final_norm_w)
    return xf.reshape(batch, seq, d)
```
